```python
import math
import jax, jax.numpy as jnp
from jax import lax
import numpy as np

D_MODEL = 2048
BATCH = 8
SEQ = 2048
DEPTH = 4

HEAD_DIM = 64
MIX_WIDTH = D_MODEL
N_HEADS_A = (MIX_WIDTH // 2) // HEAD_DIM
N_HEADS_B = (MIX_WIDTH // 2) // HEAD_DIM
WIDTH_A = N_HEADS_A * HEAD_DIM
WIDTH_B = N_HEADS_B * HEAD_DIM
IN_WIDTH = 3 * WIDTH_A + 3 * WIDTH_B
DILATED_PATTERNS = ((128, 1), (512, 4), (2048, 16))
Q_BLOCK = 128
N_EXPERTS = 32
TOP_K = 4
D_FF = D_MODEL // 2
SWIGLU_LIMIT = 7.0
SWIGLU_ALPHA = 1.702
EXPERT_BLOCK = 256
DEEPNORM_ALPHA = (2.0 * DEPTH) ** 0.25
DEEPNORM_BETA = (8.0 * DEPTH) ** -0.25
LN_EPS = 1e-5
RMS_EPS = 1e-6

kernel_name = "hybrid_dilated_stickbreak_moe_deepnorm"


def alibi_slopes(n_heads):
    return jnp.asarray([2.0 ** (-8.0 * (h + 1) / n_heads) for h in range(n_heads)], dtype=jnp.float32)


def layer_norm(x, g, b):
    x32 = x.astype(jnp.float32)
    mu = jnp.mean(x32, axis=-1, keepdims=True)
    var = jnp.mean(jnp.square(x32 - mu), axis=-1, keepdims=True)
    y = (x32 - mu) * lax.rsqrt(var + LN_EPS)
    return (y * g.astype(jnp.float32) + b.astype(jnp.float32)).astype(x.dtype)


def rms_norm(x, g):
    x32 = x.astype(jnp.float32)
    y = x32 * lax.rsqrt(jnp.mean(jnp.square(x32), axis=-1, keepdims=True) + RMS_EPS)
    return (y * g.astype(jnp.float32)).astype(x.dtype)


def dilated_pattern(q, k, v, window, dilation, slopes):
    B, S, H, Dh = q.shape
    n_back = window // dilation
    L = S // dilation
    nblk = -(-L // Q_BLOCK)
    Lp = nblk * Q_BLOCK

    def to_sub(a):
        a = a.reshape(B, L, dilation, H, Dh)
        return jnp.pad(a, ((0, 0), (0, Lp - L), (0, 0), (0, 0), (0, 0)))

    def band(a):
        a = jnp.pad(to_sub(a), ((0, 0), (Q_BLOCK, 0), (0, 0), (0, 0), (0, 0)))
        a = a.reshape(B, nblk + 1, Q_BLOCK, dilation, H, Dh)
        return jnp.concatenate([a[:, :-1], a[:, 1:]], axis=2)

    qs = to_sub(q).reshape(B, nblk, Q_BLOCK, dilation, H, Dh)
    ks, vs = band(k), band(v)
    s = jnp.einsum('bnqchd,bnkchd->bnchqk', qs, ks,
                   preferred_element_type=jnp.float32) * (1.0 / math.sqrt(Dh))
    qi = jnp.arange(Q_BLOCK)[:, None]
    kj = jnp.arange(2 * Q_BLOCK)[None, :]
    delta = qi + Q_BLOCK - kj
    blk = jnp.arange(nblk)[:, None, None]
    valid = (delta >= 0) & (delta <= n_back) & (blk * Q_BLOCK + kj - Q_BLOCK >= 0)
    bias = -slopes[:, None, None] * (dilation * delta).astype(jnp.float32)[None]
    s = jnp.where(valid[None, :, None, None], s + bias[None, None, None], -jnp.inf)
    lse = jax.nn.logsumexp(s, axis=-1)
    p = jnp.exp(s - lse[..., None])
    o = jnp.einsum('bnchqk,bnkchd->bnqchd', p.astype(v.dtype), vs)
    o = o.reshape(B, Lp * dilation, H, Dh)[:, :S]
    lse = lse.transpose(0, 1, 4, 2, 3).reshape(B, Lp * dilation, H)[:, :S]
    return o, lse


def dilated_attention(q, k, v):
    slopes = alibi_slopes(q.shape[2])
    outs, lses = [], []
    for window, dilation in DILATED_PATTERNS:
        o, lse = dilated_pattern(q, k, v, window, dilation, slopes)
        outs.append(o)
        lses.append(lse)
    w = jax.nn.softmax(jnp.stack(lses, axis=0), axis=0)
    o = jnp.sum(w[..., None] * jnp.stack(outs, axis=0).astype(jnp.float32), axis=0)
    return o.astype(q.dtype)


def stick_breaking_attention(q, k, v):
    B, S, H, Dh = q.shape
    scale = 1.0 / math.sqrt(Dh)
    outs = []
    for n in range(S // Q_BLOCK):
        t0, t1 = n * Q_BLOCK, (n + 1) * Q_BLOCK
        z = jnp.einsum('bqhd,bkhd->bhqk', q[:, t0:t1], k[:, :t1],
                       preferred_element_type=jnp.float32) * scale
        valid = jnp.arange(t1)[None, :] < (t0 + jnp.arange(Q_BLOCK))[:, None]
        log_1mb = jnp.where(valid, jax.nn.log_sigmoid(-z), 0.0)
        log_stick = lax.cumsum(log_1mb, axis=3, reverse=True) - log_1mb
        a = jnp.where(valid, jnp.exp(jax.nn.log_sigmoid(z) + log_stick), 0.0)
        outs.append(jnp.einsum('bhqk,bkhd->bqhd', a.astype(v.dtype), v[:, :t1]))
    return jnp.concatenate(outs, axis=1)


def clamped_swiglu(h):
    gate, up = h[..., :D_FF], h[..., D_FF:]
    gate = jnp.minimum(gate, SWIGLU_LIMIT)
    up = jnp.clip(up, -SWIGLU_LIMIT, SWIGLU_LIMIT)
    return (up + 1.0) * (gate * jax.nn.sigmoid(gate * SWIGLU_ALPHA))


def moe_ffn(x, w_router, b_router, w_gate_up, b_gate_up, w_down, b_down):
    B, S, D = x.shape
    xt = x.reshape(-1, D)
    T = xt.shape[0]
    n_assign = T * TOP_K
    logits = (xt @ w_router + b_router).astype(jnp.float32)
    top_val, top_idx = lax.top_k(logits, TOP_K)
    gates = jax.nn.softmax(top_val, axis=-1).astype(x.dtype)
    e_flat = top_idx.reshape(-1)
    order = jnp.argsort(e_flat)
    e_sorted = e_flat[order]
    tok_sorted = order // TOP_K
    gate_sorted = gates.reshape(-1)[order]
    counts = jnp.bincount(e_flat, length=N_EXPERTS)
    padded = (counts + EXPERT_BLOCK - 1) // EXPERT_BLOCK * EXPERT_BLOCK
    start = jnp.cumsum(counts) - counts
    pend = jnp.cumsum(padded)
    pstart = pend - padded
    dest = pstart[e_sorted] + (jnp.arange(n_assign) - start[e_sorted])
    n_blocks = -(-n_assign // EXPERT_BLOCK) + N_EXPERTS
    block_expert = jnp.minimum(
        jnp.searchsorted(pend, jnp.arange(n_blocks) * EXPERT_BLOCK, side='right'),
        N_EXPERTS - 1).astype(jnp.int32)
    x_buf = jnp.zeros((n_blocks * EXPERT_BLOCK, D), x.dtype).at[dest].set(xt[tok_sorted])

    def expert_block(args):
        xb, e = args
        h = xb @ w_gate_up[e] + b_gate_up[e]
        return clamped_swiglu(h) @ w_down[e] + b_down[e]

    y_buf = lax.map(expert_block, (x_buf.reshape(n_blocks, EXPERT_BLOCK, D), block_expert))
    y_rows = y_buf.reshape(-1, D)[dest] * gate_sorted[:, None]
    y = jnp.zeros_like(xt).at[tok_sorted].add(y_rows)
    return y.reshape(B, S, D)


def setup_inputs(seed: int = 0) -> dict:
    key = jax.random.key(seed)
    ks = jax.random.split(key, 20)
    f32 = jnp.float32
    std_in = D_MODEL ** -0.5
    x = jax.random.normal(ks[0], (BATCH, SEQ, D_MODEL), f32)
    col_scale = jnp.concatenate([
        jnp.ones((2 * WIDTH_A,), f32), jnp.full((WIDTH_A,), DEEPNORM_BETA, f32),
        jnp.ones((2 * WIDTH_B,), f32), jnp.full((WIDTH_B,), DEEPNORM_BETA, f32)])
    w_in = jax.random.normal(ks[1], (DEPTH, D_MODEL, IN_WIDTH), f32) * std_in * col_scale
    g_mix_a = 1.0 + 0.02 * jax.random.normal(ks[2], (DEPTH, WIDTH_A), f32)
    g_mix_b = 1.0 + 0.02 * jax.random.normal(ks[3], (DEPTH, WIDTH_B), f32)
    w_out = jax.random.normal(ks[4], (DEPTH, MIX_WIDTH, D_MODEL), f32) * (MIX_WIDTH ** -0.5) * DEEPNORM_BETA
    ln1_g = 1.0 + 0.02 * jax.random.normal(ks[5], (DEPTH, D_MODEL), f32)
    ln1_b = 0.02 * jax.random.normal(ks[6], (DEPTH, D_MODEL), f32)
    w_router = jax.random.normal(ks[7], (DEPTH, D_MODEL, N_EXPERTS), f32) * std_in
    b_router = 0.01 * jax.random.normal(ks[8], (DEPTH, N_EXPERTS), f32)
    w_gate_up = jax.random.normal(ks[9], (DEPTH, N_EXPERTS, D_MODEL, 2 * D_FF), f32) * std_in
    b_gate_up = 0.02 * jax.random.normal(ks[10], (DEPTH, N_EXPERTS, 2 * D_FF), f32)
    w_down = jax.random.normal(ks[11], (DEPTH, N_EXPERTS, D_FF, D_MODEL), f32) * (D_FF ** -0.5) * DEEPNORM_BETA
    b_down = 0.02 * jax.random.normal(ks[12], (DEPTH, N_EXPERTS, D_MODEL), f32)
    ln2_g = 1.0 + 0.02 * jax.random.normal(ks[13], (DEPTH, D_MODEL), f32)
    ln2_b = 0.02 * jax.random.normal(ks[14], (DEPTH, D_MODEL), f32)
    return {"x": x, "w_in": w_in, "g_mix_a": g_mix_a, "g_mix_b": g_mix_b, "w_out": w_out,
            "ln1_g": ln1_g, "ln1_b": ln1_b, "w_router": w_router, "b_router": b_router,
            "w_gate_up": w_gate_up, "b_gate_up": b_gate_up, "w_down": w_down, "b_down": b_down,
            "ln2_g": ln2_g, "ln2_b": ln2_b}


def reference(x, w_in, g_mix_a, g_mix_b, w_out, ln1_g, ln1_b, w_router, b_router,
              w_gate_up, b_gate_up, w_down, b_down, ln2_g, ln2_b):
    B, S, _ = x.shape
    splits = np.cumsum([WIDTH_A, WIDTH_A, WIDTH_A, WIDTH_B, WIDTH_B])
    for l in range(DEPTH):
        h = x @ w_in[l]
        qa, ka, va, qb, kb, vb = jnp.split(h, splits, axis=-1)
        heads_a = lambda t: t.reshape(B, S, N_HEADS_A, HEAD_DIM)
        heads_b = lambda t: t.reshape(B, S, N_HEADS_B, HEAD_DIM)
        oa = dilated_attention(heads_a(qa), heads_a(ka), heads_a(va)).reshape(B, S, WIDTH_A)
        ob = stick_breaking_attention(heads_b(qb), heads_b(kb), heads_b(vb)).reshape(B, S, WIDTH_B)
        mix = jnp.concatenate([rms_norm(oa, g_mix_a[l]), rms_norm(ob, g_mix_b[l])], axis=-1)
        x = layer_norm(DEEPNORM_ALPHA * x + mix @ w_out[l], ln1_g[l], ln1_b[l])
        y = moe_ffn(x, w_router[l], b_router[l], w_gate_up[l], b_gate_up[l], w_down[l], b_down[l])
        x = layer_norm(DEEPNORM_ALPHA * x + y, ln2_g[l], ln2_b[l])
    return x
```

```python
import functools
import math

import jax
import jax.numpy as jnp
from jax import lax
from jax.experimental import pallas as pl
from jax.experimental.pallas import tpu as pltpu

HEAD_DIM = 64
HEADS_PER_BLOCK = 2
LANES = HEADS_PER_BLOCK * HEAD_DIM
Q_BLOCK = 128
DILATED_PATTERNS = ((128, 1), (512, 4), (2048, 16))
N_EXPERTS = 32
TOP_K = 4
SWIGLU_LIMIT = 7.0
SWIGLU_ALPHA = 1.702
LN_EPS = 1e-5
RMS_EPS = 1e-6
MASKED = -1e30

VMEM_LIMIT = 56 * 1024 * 1024
PROJ_ROWS = 256
ROUTER_ROWS = 256
DISPATCH_ROWS = 128
EXPERT_ROWS = 256
COMBINE_ROWS = 64

f32 = jnp.float32
bf16 = jnp.bfloat16


def _params(n_grid_dims):
    return pltpu.CompilerParams(
        dimension_semantics=("arbitrary",) * n_grid_dims, vmem_limit_bytes=VMEM_LIMIT)


def _dot_nt(a, b):
    return lax.dot_general(a, b, (((1,), (1,)), ((), ())), preferred_element_type=f32)


def _dot(a, b):
    return jnp.dot(a, b, preferred_element_type=f32)


def _proj_kernel(x_ref, w_ref, o_ref):
    o_ref[...] = _dot(x_ref[...].astype(bf16), w_ref[...]).astype(o_ref.dtype)


def _proj(x, w, out_dtype):
    T, K = x.shape
    N = w.shape[1]
    tm = min(PROJ_ROWS, T)
    return pl.pallas_call(
        _proj_kernel,
        name="proj",
        grid=(T // tm,),
        in_specs=[pl.BlockSpec((tm, K), lambda i: (i, 0)),
                  pl.BlockSpec((K, N), lambda i: (0, 0))],
        out_specs=pl.BlockSpec((tm, N), lambda i: (i, 0)),
        out_shape=jax.ShapeDtypeStruct((T, N), out_dtype),
        compiler_params=_params(1),
    )(x, w)


def _dilated_kernel(slopes_ref, q_ref, k_ref, v_ref, o_ref, acc_o, acc_m, acc_l):
    S = q_ref.shape[0]
    hp = pl.program_id(1)
    row = lax.broadcasted_iota(jnp.int32, (Q_BLOCK, Q_BLOCK), 0)
    col = lax.broadcasted_iota(jnp.int32, (Q_BLOCK, Q_BLOCK), 1)
    head0 = col < HEAD_DIM
    delta_cur = (row - col).astype(f32)
    delta_prev = (row - col + Q_BLOCK).astype(f32)
    valid_cur = row >= col
    valid_prev = col >= row

    for r, (window, d) in enumerate(DILATED_PATTERNS):
        assert window // d == Q_BLOCK
        nblk = (S // d) // Q_BLOCK
        has_prev_blocks = nblk > 1
        coefs = [-(slopes_ref[hp * HEADS_PER_BLOCK + j] * float(d)) for j in range(HEADS_PER_BLOCK)]

        def block(idx, carry, d=d, nblk=nblk, r=r, coefs=coefs, has_prev_blocks=has_prev_blocks):
            c = idx // nblk
            nb = idx % nblk
            start = nb * (Q_BLOCK * d) + c
            rows = pl.ds(start, Q_BLOCK, stride=d) if d > 1 else pl.ds(start, Q_BLOCK)
            q = (q_ref[rows, :] * (1.0 / math.sqrt(HEAD_DIM))).astype(bf16)
            kc = k_ref[rows, :].astype(bf16)
            vc = v_ref[rows, :].astype(bf16)
            if has_prev_blocks:
                pstart = jnp.maximum(nb - 1, 0) * (Q_BLOCK * d) + c
                prows = pl.ds(pstart, Q_BLOCK, stride=d) if d > 1 else pl.ds(pstart, Q_BLOCK)
                kp = k_ref[prows, :].astype(bf16)
                vp = v_ref[prows, :].astype(bf16)
                prev_ok = jnp.logical_and(valid_prev, nb > 0)
            outs, maxs, sums = [], [], []
            for j in range(HEADS_PER_BLOCK):
                qm = jnp.where(head0 if j == 0 else jnp.logical_not(head0), q, jnp.zeros_like(q))
                s_c = jnp.where(valid_cur, _dot_nt(qm, kc) + coefs[j] * delta_cur, MASKED)
                m = jnp.max(s_c, axis=-1, keepdims=True)
                if has_prev_blocks:
                    s_p = jnp.where(prev_ok, _dot_nt(qm, kp) + coefs[j] * delta_prev, MASKED)
                    m = jnp.maximum(m, jnp.max(s_p, axis=-1, keepdims=True))
                p_c = jnp.exp(s_c - m)
                l = jnp.sum(p_c, axis=-1, keepdims=True)
                o = _dot(p_c.astype(bf16), vc)
                if has_prev_blocks:
                    p_p = jnp.exp(s_p - m)
                    l = l + jnp.sum(p_p, axis=-1, keepdims=True)
                    o = o + _dot(p_p.astype(bf16), vp)
                outs.append(o)
                maxs.append(m)
                sums.append(l)
            acc_o[r, rows, :] = jnp.where(head0, outs[0], outs[1])
            acc_m[r, rows, :] = jnp.where(head0, maxs[0], maxs[1])
            acc_l[r, rows, :] = jnp.where(head0, sums[0], sums[1])
            return carry

        lax.fori_loop(0, d * nblk, block, 0)

    def merge(i, carry):
        rows = pl.ds(pl.multiple_of(i * Q_BLOCK, Q_BLOCK), Q_BLOCK)
        ms = [acc_m[r, rows, :] for r in range(len(DILATED_PATTERNS))]
        m_all = functools.reduce(jnp.maximum, ms)
        num = jnp.zeros((Q_BLOCK, LANES), f32)
        den = jnp.zeros((Q_BLOCK, LANES), f32)
        for r in range(len(DILATED_PATTERNS)):
            w = jnp.exp(ms[r] - m_all)
            num = num + w * acc_o[r, rows, :]
            den = den + w * acc_l[r, rows, :]
        o_ref[rows, :] = num / den
        return carry

    lax.fori_loop(0, S // Q_BLOCK, merge, 0)


def _dilated_attention(ha, slopes, n_heads):
    B, S, _ = ha.shape
    nhp = n_heads // HEADS_PER_BLOCK
    blk = lambda off: pl.BlockSpec((None, S, LANES), lambda b, h, sl, off=off: (b, 0, off + h))
    return pl.pallas_call(
        _dilated_kernel,
        name="dilated_attn",
        grid_spec=pltpu.PrefetchScalarGridSpec(
            num_scalar_prefetch=1,
            grid=(B, nhp),
            in_specs=[blk(0), blk(nhp), blk(2 * nhp)],
            out_specs=pl.BlockSpec((None, S, LANES), lambda b, h, sl: (b, 0, h)),
            scratch_shapes=[pltpu.VMEM((len(DILATED_PATTERNS), S, LANES), f32)] * 3),
        out_shape=jax.ShapeDtypeStruct((B, S, n_heads * HEAD_DIM), f32),
        compiler_params=_params(2),
    )(slopes, ha, ha, ha)


def _stick_kernel(q_ref, k_ref, v_ref, o_ref):
    S = q_ref.shape[0]
    row = lax.broadcasted_iota(jnp.int32, (Q_BLOCK, Q_BLOCK), 0)
    col = lax.broadcasted_iota(jnp.int32, (Q_BLOCK, Q_BLOCK), 1)
    head0 = col < HEAD_DIM
    strictly_causal = col < row
    later_keys = (row > col).astype(bf16)

    def q_block(n, carry):
        qrows = pl.ds(pl.multiple_of(n * Q_BLOCK, Q_BLOCK), Q_BLOCK)
        q = q_ref[qrows, :] * (1.0 / math.sqrt(HEAD_DIM))
        outs = []
        for j in range(HEADS_PER_BLOCK):
            qm = jnp.where(head0 if j == 0 else jnp.logical_not(head0), q, jnp.zeros_like(q))

            def k_block(i, state, qm=qm):
                o, nearer = state
                krows = pl.ds(pl.multiple_of((n - i) * Q_BLOCK, Q_BLOCK), Q_BLOCK)
                z = _dot_nt(qm, k_ref[krows, :])
                log_b = jnp.minimum(z, 0.0) - jnp.log1p(jnp.exp(-jnp.abs(z)))
                valid = jnp.logical_or(strictly_causal, i > 0)
                log_1mb = jnp.where(valid, log_b - z, 0.0)
                hi = log_1mb.astype(bf16)
                lo = (log_1mb - hi.astype(f32)).astype(bf16)
                within = _dot(hi, later_keys) + _dot(lo, later_keys)
                a = jnp.where(valid, jnp.exp(log_b + within + nearer), 0.0)
                o = o + _dot(a.astype(bf16), v_ref[krows, :])
                nearer = nearer + jnp.sum(log_1mb, axis=-1, keepdims=True)
                return o, nearer

            o, _ = lax.fori_loop(
                0, n + 1, k_block,
                (jnp.zeros((Q_BLOCK, LANES), f32), jnp.zeros((Q_BLOCK, 1), f32)))
            outs.append(o)
        o_ref[qrows, :] = jnp.where(head0, outs[0], outs[1])
        return carry

    lax.fori_loop(0, S // Q_BLOCK, q_block, 0)


def _stick_attention(hb, n_heads):
    B, S, _ = hb.shape
    nhp = n_heads // HEADS_PER_BLOCK
    blk = lambda off: pl.BlockSpec((None, S, LANES), lambda b, h, off=off: (b, 0, off + h))
    return pl.pallas_call(
        _stick_kernel,
        name="stick_attn",
        grid=(B, nhp),
        in_specs=[blk(0), blk(nhp), blk(2 * nhp)],
        out_specs=pl.BlockSpec((None, S, LANES), lambda b, h: (b, 0, h)),
        out_shape=jax.ShapeDtypeStruct((B, S, n_heads * HEAD_DIM), f32),
        compiler_params=_params(2),
    )(hb, hb, hb)


def _layer_norm(u, g, b):
    mu = jnp.mean(u, axis=-1, keepdims=True)
    var = jnp.mean(jnp.square(u - mu), axis=-1, keepdims=True)
    return (u - mu) * lax.rsqrt(var + LN_EPS) * g + b


def _rms_norm(o, g):
    return o * lax.rsqrt(jnp.mean(jnp.square(o), axis=-1, keepdims=True) + RMS_EPS) * g


def _mix_router_kernel(alpha, oa_ref, ob_ref, x_ref, wo_ref, ga_ref, gb_ref, lg_ref, lb_ref,
                       wr_ref, br_ref, x1_ref, idx_ref, gate_ref, pos_ref, cnt_ref, running):
    tm = x_ref.shape[0]
    wa = oa_ref.shape[1]

    @pl.when(pl.program_id(0) == 0)
    def _():
        running[...] = jnp.zeros_like(running)

    mix_a = _rms_norm(oa_ref[...], ga_ref[...]).astype(bf16)
    mix_b = _rms_norm(ob_ref[...], gb_ref[...]).astype(bf16)
    y = _dot(mix_a, wo_ref[:wa, :]) + _dot(mix_b, wo_ref[wa:, :])
    x1 = _layer_norm(alpha * x_ref[...] + y, lg_ref[...], lb_ref[...])
    x1_ref[...] = x1

    logits = jnp.dot(x1, wr_ref[...], preferred_element_type=f32,
                     precision=lax.Precision.HIGHEST) + br_ref[...]
    lane = lax.broadcasted_iota(jnp.int32, (tm, N_EXPERTS), 1)
    work = logits
    vals, sels, idxs = [], [], []
    for _ in range(TOP_K):
        m = jnp.max(work, axis=-1, keepdims=True)
        idx = jnp.min(jnp.where(work == m, lane, N_EXPERTS), axis=-1, keepdims=True)
        sel = lane == idx
        work = jnp.where(sel, -jnp.inf, work)
        vals.append(m)
        sels.append(sel)
        idxs.append(idx)
    exps = [jnp.exp(v - vals[0]) for v in vals]
    denom = functools.reduce(jnp.add, exps)

    chosen = functools.reduce(jnp.logical_or, sels)
    trow = lax.broadcasted_iota(jnp.int32, (tm, tm), 0)
    tcol = lax.broadcasted_iota(jnp.int32, (tm, tm), 1)
    earlier = (tcol < trow).astype(bf16)
    rank = _dot(earlier, chosen.astype(bf16)) + running[...]
    running[...] = running[...] + jnp.sum(chosen.astype(f32), axis=0, keepdims=True)
    cnt_ref[...] = running[...].astype(jnp.int32)

    out_lane = lax.broadcasted_iota(jnp.int32, (tm, LANES), 1)
    idx_out = jnp.zeros((tm, LANES), jnp.int32)
    gate_out = jnp.zeros((tm, LANES), f32)
    pos_out = jnp.zeros((tm, LANES), jnp.int32)
    for k in range(TOP_K):
        pos = jnp.sum(jnp.where(sels[k], rank, 0.0), axis=-1, keepdims=True).astype(jnp.int32)
        idx_out = jnp.where(out_lane == k, idxs[k], idx_out)
        gate_out = jnp.where(out_lane == k, exps[k] / denom, gate_out)
        pos_out = jnp.where(out_lane == k, pos, pos_out)
    idx_ref[...] = idx_out
    gate_ref[...] = gate_out
    pos_ref[...] = pos_out


def _mix_router(oa, ob, x, wo, ga, gb, lg, lb, wr, br, alpha):
    T, D = x.shape
    W = oa.shape[1]
    tm = min(ROUTER_ROWS, T)
    rows = lambda n: pl.BlockSpec((tm, n), lambda i: (i, 0))
    full = lambda a: pl.BlockSpec(a.shape, lambda i: (0,) * a.ndim)
    return pl.pallas_call(
        functools.partial(_mix_router_kernel, alpha),
        name="mix_router",
        grid=(T // tm,),
        in_specs=[rows(W), rows(W), rows(D), full(wo), full(ga), full(gb), full(lg), full(lb),
                  full(wr), full(br)],
        out_specs=[rows(D), rows(LANES), rows(LANES), rows(LANES),
                   pl.BlockSpec((1, N_EXPERTS), lambda i: (0, 0))],
        out_shape=[jax.ShapeDtypeStruct((T, D), f32),
                   jax.ShapeDtypeStruct((T, LANES), jnp.int32),
                   jax.ShapeDtypeStruct((T, LANES), f32),
                   jax.ShapeDtypeStruct((T, LANES), jnp.int32),
                   jax.ShapeDtypeStruct((1, N_EXPERTS), jnp.int32)],
        scratch_shapes=[pltpu.VMEM((1, N_EXPERTS), f32)],
        compiler_params=_params(1),
    )(oa, ob, x, wo, ga, gb, lg, lb, wr, br)


def _dispatch_kernel(dest_ref, x_ref, buf_in_ref, buf_ref, sem):
    del buf_in_ref
    tm = x_ref.shape[0]
    base = pl.program_id(0) * (tm * TOP_K)

    def row_copy(a):
        return pltpu.make_async_copy(
            x_ref.at[pl.ds(a // TOP_K, 1), :], buf_ref.at[pl.ds(dest_ref[base + a], 1), :], sem)

    def start(a, carry):
        row_copy(a).start()
        return carry

    def wait(a, carry):
        row_copy(a).wait()
        return carry

    lax.fori_loop(0, tm * TOP_K, start, 0)
    lax.fori_loop(0, tm * TOP_K, wait, 0)


def _dispatch(x1, dest_flat, buf_zero):
    T, D = x1.shape
    tm = min(DISPATCH_ROWS, T)
    return pl.pallas_call(
        _dispatch_kernel,
        name="moe_dispatch",
        grid_spec=pltpu.PrefetchScalarGridSpec(
            num_scalar_prefetch=1,
            grid=(T // tm,),
            in_specs=[pl.BlockSpec((tm, D), lambda i, d: (i, 0)),
                      pl.BlockSpec(memory_space=pl.ANY)],
            out_specs=pl.BlockSpec(memory_space=pl.ANY),
            scratch_shapes=[pltpu.SemaphoreType.DMA(())]),
        out_shape=jax.ShapeDtypeStruct(buf_zero.shape, buf_zero.dtype),
        input_output_aliases={2: 0},
        compiler_params=_params(1),
    )(dest_flat, x1, buf_zero)


def _expert_kernel(blk_expert_ref, n_used_ref, x_ref, wgu_ref, bgu_ref, wd_ref, bd_ref, y_ref):
    del blk_expert_ref
    d_ff = wd_ref.shape[0]

    @pl.when(pl.program_id(0) < n_used_ref[0])
    def _():
        h = _dot(x_ref[...].astype(bf16), wgu_ref[...]) + bgu_ref[...]
        gate = jnp.minimum(h[:, :d_ff], SWIGLU_LIMIT)
        up = jnp.clip(h[:, d_ff:], -SWIGLU_LIMIT, SWIGLU_LIMIT)
        act = (up + 1.0) * (gate * jax.nn.sigmoid(gate * SWIGLU_ALPHA))
        y_ref[...] = _dot(act.astype(bf16), wd_ref[...]) + bd_ref[...]

    @pl.when(pl.program_id(0) >= n_used_ref[0])
    def _():
        y_ref[...] = jnp.zeros_like(y_ref)


def _expert_ffn(x_buf, blk_expert, n_used, wgu, bgu, wd, bd):
    R, D = x_buf.shape
    E, _, F2 = wgu.shape
    F = wd.shape[1]
    tm = EXPERT_ROWS
    used = lambda i, be, nu: jnp.minimum(i, nu[0] - 1)
    return pl.pallas_call(
        _expert_kernel,
        name="expert_ffn",
        grid_spec=pltpu.PrefetchScalarGridSpec(
            num_scalar_prefetch=2,
            grid=(R // tm,),
            in_specs=[pl.BlockSpec((tm, D), lambda i, be, nu: (used(i, be, nu), 0)),
                      pl.BlockSpec((None, D, F2), lambda i, be, nu: (be[i], 0, 0)),
                      pl.BlockSpec((None, 1, F2), lambda i, be, nu: (be[i], 0, 0)),
                      pl.BlockSpec((None, F, D), lambda i, be, nu: (be[i], 0, 0)),
                      pl.BlockSpec((None, 1, D), lambda i, be, nu: (be[i], 0, 0))],
            out_specs=pl.BlockSpec((tm, D), lambda i, be, nu: (i, 0))),
        out_shape=jax.ShapeDtypeStruct((R, D), f32),
        compiler_params=_params(1),
    )(blk_expert, n_used, x_buf, wgu, bgu.reshape(E, 1, F2), wd, bd.reshape(E, 1, D))


def _combine_kernel(alpha, dest_ref, y_hbm, x_ref, gate_ref, lg_ref, lb_ref, o_ref, rows_buf, sems):
    tm = x_ref.shape[0]
    i = pl.program_id(0)
    n = pl.num_programs(0)

    def row_copy(tile, slot, a):
        return pltpu.make_async_copy(
            y_hbm.at[pl.ds(dest_ref[tile * (tm * TOP_K) + a], 1), :],
            rows_buf.at[slot, a % TOP_K, pl.ds(a // TOP_K, 1), :],
            sems.at[slot])

    def start_tile(tile, slot):
        def body(a, carry):
            row_copy(tile, slot, a).start()
            return carry
        lax.fori_loop(0, tm * TOP_K, body, 0)

    slot = i % 2

    @pl.when(i == 0)
    def _():
        start_tile(0, 0)

    @pl.when(i + 1 < n)
    def _():
        start_tile(i + 1, 1 - slot)

    def wait_body(a, carry):
        row_copy(i, slot, a).wait()
        return carry
    lax.fori_loop(0, tm * TOP_K, wait_body, 0)

    y = jnp.zeros(x_ref.shape, f32)
    for k in range(TOP_K):
        y = y + gate_ref[:, k:k + 1] * rows_buf[slot, k]
    o_ref[...] = _layer_norm(alpha * x_ref[...] + y, lg_ref[...], lb_ref[...])


def _combine(y_buf, dest_flat, x1, gates, lg, lb, alpha):
    T, D = x1.shape
    tm = min(COMBINE_ROWS, T)
    return pl.pallas_call(
        functools.partial(_combine_kernel, alpha),
        name="moe_combine",
        grid_spec=pltpu.PrefetchScalarGridSpec(
            num_scalar_prefetch=1,
            grid=(T // tm,),
            in_specs=[pl.BlockSpec(memory_space=pl.ANY),
                      pl.BlockSpec((tm, D), lambda i, d: (i, 0)),
                      pl.BlockSpec((tm, LANES), lambda i, d: (i, 0)),
                      pl.BlockSpec((1, D), lambda i, d: (0, 0)),
                      pl.BlockSpec((1, D), lambda i, d: (0, 0))],
            out_specs=pl.BlockSpec((tm, D), lambda i, d: (i, 0)),
            scratch_shapes=[pltpu.VMEM((2, TOP_K, tm, D), f32),
                            pltpu.SemaphoreType.DMA((2,))]),
        out_shape=jax.ShapeDtypeStruct((T, D), f32),
        compiler_params=_params(1),
    )(dest_flat, y_buf, x1, gates, lg, lb)


def _moe(x1, idx, gates, pos, counts, wgu, bgu, wd, bd, lg, lb, alpha):
    T, D = x1.shape
    n_assign = T * TOP_K
    counts = counts.reshape(N_EXPERTS)
    padded = (counts + EXPERT_ROWS - 1) // EXPERT_ROWS * EXPERT_ROWS
    group_end = jnp.cumsum(padded)
    group_start = group_end - padded
    dest = (group_start[idx[:, :TOP_K]] + pos[:, :TOP_K]).reshape(n_assign).astype(jnp.int32)
    n_blocks = -(-n_assign // EXPERT_ROWS) + N_EXPERTS
    blk_expert = jnp.minimum(
        jnp.searchsorted(group_end, jnp.arange(n_blocks) * EXPERT_ROWS, side="right"),
        N_EXPERTS - 1).astype(jnp.int32)
    n_used = (group_end[-1:] // EXPERT_ROWS).astype(jnp.int32)
    x_buf = _dispatch(x1, dest, jnp.zeros((n_blocks * EXPERT_ROWS, D), f32))
    y_buf = _expert_ffn(x_buf, blk_expert, n_used, wgu, bgu, wd, bd)
    return _combine(y_buf, dest, x1, gates, lg, lb, alpha)


def kernel(x, w_in, g_mix_a, g_mix_b, w_out, ln1_g, ln1_b, w_router, b_router,
           w_gate_up, b_gate_up, w_down, b_down, ln2_g, ln2_b):
    B, S, D = x.shape
    depth = w_in.shape[0]
    wa = g_mix_a.shape[1]
    wb = g_mix_b.shape[1]
    heads_a = wa // HEAD_DIM
    heads_b = wb // HEAD_DIM
    alpha = (2.0 * depth) ** 0.25
    slopes = jnp.asarray([2.0 ** (-8.0 * (h + 1) / heads_a) for h in range(heads_a)], f32)
    row = lambda a: a.reshape(1, -1)

    xt = x.reshape(B * S, D)
    for l in range(depth):
        w_l = w_in[l].astype(bf16)
        ha = _proj(xt, w_l[:, :3 * wa], f32).reshape(B, S, 3 * wa)
        hb = _proj(xt, w_l[:, 3 * wa:], bf16).reshape(B, S, 3 * wb)
        oa = _dilated_attention(ha, slopes, heads_a).reshape(B * S, wa)
        ob = _stick_attention(hb, heads_b).reshape(B * S, wb)
        x1, idx, gates, pos, counts = _mix_router(
            oa, ob, xt, w_out[l].astype(bf16), row(g_mix_a[l]), row(g_mix_b[l]),
            row(ln1_g[l]), row(ln1_b[l]), w_router[l], row(b_router[l]), alpha)
        xt = _moe(x1, idx, gates, pos, counts, w_gate_up[l].astype(bf16), b_gate_up[l],
                  w_down[l].astype(bf16), b_down[l], row(ln2_g[l]), row(ln2_b[l]), alpha)
    return xt.reshape(B, S, D)
```

```python
import functools
import math

import jax
import jax.numpy as jnp
from jax import lax
from jax.experimental import pallas as pl
from jax.experimental.pallas import tpu as pltpu

HEAD_DIM = 64
HEADS_PER_BLOCK = 2
LANES = HEADS_PER_BLOCK * HEAD_DIM
Q_BLOCK = 128
DILATED_PATTERNS = ((128, 1), (512, 4), (2048, 16))
N_EXPERTS = 32
TOP_K = 4
SWIGLU_LIMIT = 7.0
SWIGLU_ALPHA = 1.702
LN_EPS = 1e-5
RMS_EPS = 1e-6
MASKED = -1e30

VMEM_LIMIT = 56 * 1024 * 1024
PROJ_ROWS = 256
ROUTER_ROWS = 256
DISPATCH_ROWS = 128
EXPERT_ROWS = 256
COMBINE_ROWS = 128
STICK_TILE = 256

f32 = jnp.float32
bf16 = jnp.bfloat16


def _params(n_grid_dims):
    return pltpu.CompilerParams(
        dimension_semantics=("arbitrary",) * n_grid_dims, vmem_limit_bytes=VMEM_LIMIT)


def _dot_nt(a, b):
    return lax.dot_general(a, b, (((1,), (1,)), ((), ())), preferred_element_type=f32)


def _dot(a, b):
    return jnp.dot(a, b, preferred_element_type=f32)


def _proj_kernel(x_ref, w_ref, o_ref):
    o_ref[...] = _dot(x_ref[...].astype(bf16), w_ref[...]).astype(o_ref.dtype)


def _proj(x, w, layer, half, out_dtype):
    T, K = x.shape
    N = w.shape[2] // 2
    tm = min(PROJ_ROWS, T)
    return pl.pallas_call(
        _proj_kernel,
        name="proj",
        grid=(T // tm,),
        in_specs=[pl.BlockSpec((tm, K), lambda i: (i, 0)),
                  pl.BlockSpec((None, K, N), lambda i: (layer, 0, half))],
        out_specs=pl.BlockSpec((tm, N), lambda i: (i, 0)),
        out_shape=jax.ShapeDtypeStruct((T, N), out_dtype),
        compiler_params=_params(1),
    )(x, w)


def _dilated_kernel(slopes_ref, q_ref, k_ref, v_ref, o_ref, acc_o, acc_m, acc_l, bias_ref):
    S = q_ref.shape[0]
    QB = Q_BLOCK
    hp = pl.program_id(1)
    head0 = lax.broadcasted_iota(jnp.int32, (QB, LANES), 1) < HEAD_DIM

    def stack_heads(x):
        first = lax.broadcasted_iota(jnp.int32, x.shape, 1) < HEAD_DIM
        return jnp.concatenate([jnp.where(first, x, jnp.zeros_like(x)),
                                jnp.where(first, jnp.zeros_like(x), x)], axis=0)

    for r, (window, d) in enumerate(DILATED_PATTERNS):
        assert window // d == QB
        nblk = (S // d) // QB
        nk = 2 * QB if nblk > 1 else QB
        srow = lax.broadcasted_iota(jnp.int32, (2 * QB, nk), 0)
        scol = lax.broadcasted_iota(jnp.int32, (2 * QB, nk), 1)
        second = srow >= QB
        coef = jnp.where(second, -(slopes_ref[hp * HEADS_PER_BLOCK + 1] * float(d)),
                         -(slopes_ref[hp * HEADS_PER_BLOCK] * float(d)))
        for variant, offset in enumerate((nk - QB, 0)):
            delta = jnp.where(second, srow - QB, srow) - scol + offset
            ok = jnp.logical_and(delta >= 0, delta <= QB)
            bias_ref[variant, :, :nk] = jnp.where(ok, coef * delta.astype(f32), MASKED)

        def block(idx, carry, d=d, nblk=nblk, r=r, nk=nk):
            c = idx // nblk
            nb = idx % nblk
            strided = lambda start, n: pl.ds(start, n, stride=d) if d > 1 else pl.ds(start, n)
            rows = strided(nb * (QB * d) + c, QB)
            krows = strided(jnp.maximum(nb - (nk // QB - 1), 0) * (QB * d) + c, nk)
            qs = stack_heads((q_ref[rows, :] * (1.0 / math.sqrt(HEAD_DIM))).astype(bf16))
            k = k_ref[krows, :].astype(bf16)
            v = v_ref[krows, :].astype(bf16)
            variant = jnp.where(nb > 0, 0, 1) if nblk > 1 else 0
            s = _dot_nt(qs, k) + bias_ref[variant, :, :nk]
            m = jnp.max(s, axis=-1, keepdims=True)
            p = jnp.exp(s - m)
            l = jnp.sum(p, axis=-1, keepdims=True)
            p = p.astype(bf16)
            p2 = jnp.concatenate([p[:QB], p[QB:]], axis=1)
            acc_o[r, rows, :] = _dot(p2, stack_heads(v))
            acc_m[r, rows, :] = jnp.where(head0, m[:QB], m[QB:])
            acc_l[r, rows, :] = jnp.where(head0, l[:QB], l[QB:])
            return carry

        lax.fori_loop(0, d * nblk, block, 0, unroll=2)

    def merge(i, carry):
        rows = pl.ds(pl.multiple_of(i * Q_BLOCK, Q_BLOCK), Q_BLOCK)
        ms = [acc_m[r, rows, :] for r in range(len(DILATED_PATTERNS))]
        m_all = functools.reduce(jnp.maximum, ms)
        num = jnp.zeros((Q_BLOCK, LANES), f32)
        den = jnp.zeros((Q_BLOCK, LANES), f32)
        for r in range(len(DILATED_PATTERNS)):
            w = jnp.exp(ms[r] - m_all)
            num = num + w * acc_o[r, rows, :]
            den = den + w * acc_l[r, rows, :]
        o_ref[rows, :] = num / den
        return carry

    lax.fori_loop(0, S // Q_BLOCK, merge, 0)


def _dilated_attention(ha, slopes, n_heads):
    B, S, _ = ha.shape
    nhp = n_heads // HEADS_PER_BLOCK
    blk = lambda off: pl.BlockSpec((None, S, LANES), lambda b, h, sl, off=off: (b, 0, off + h))
    return pl.pallas_call(
        _dilated_kernel,
        name="dilated_attn",
        grid_spec=pltpu.PrefetchScalarGridSpec(
            num_scalar_prefetch=1,
            grid=(B, nhp),
            in_specs=[blk(0), blk(nhp), blk(2 * nhp)],
            out_specs=pl.BlockSpec((None, S, LANES), lambda b, h, sl: (b, 0, h)),
            scratch_shapes=[pltpu.VMEM((len(DILATED_PATTERNS), S, LANES), f32)] * 3
            + [pltpu.VMEM((2, 2 * Q_BLOCK, 2 * Q_BLOCK), f32)]),
        out_shape=jax.ShapeDtypeStruct((B, S, n_heads * HEAD_DIM), f32),
        compiler_params=_params(2),
    )(slopes, ha, ha, ha)


def _stick_kernel(q_ref, k_ref, v_ref, o_ref, later_ref):
    S = q_ref.shape[0]
    T = min(STICK_TILE, S)
    lane = lax.broadcasted_iota(jnp.int32, (T, LANES), 1)
    head0 = lane < HEAD_DIM
    krow = lax.broadcasted_iota(jnp.int32, (T, T), 0)
    kcol = lax.broadcasted_iota(jnp.int32, (T, T), 1)
    later_ref[...] = (krow > kcol).astype(bf16)
    srow = lax.broadcasted_iota(jnp.int32, (2 * T, T), 0)
    scol = lax.broadcasted_iota(jnp.int32, (2 * T, T), 1)
    causal = scol < jnp.where(srow >= T, srow - T, srow)

    def stack_heads(x):
        return jnp.concatenate([jnp.where(head0, x, jnp.zeros_like(x)),
                                jnp.where(head0, jnp.zeros_like(x), x)], axis=0)

    def tile(qs, kb, nearer, acc, diagonal):
        krows = pl.ds(pl.multiple_of(kb * T, T), T)
        z = _dot_nt(qs, k_ref[krows, :])
        log_b = jnp.minimum(z, 0.0) - jnp.log(1.0 + jnp.exp(-jnp.abs(z)))
        log_1mb = log_b - z
        if diagonal:
            log_1mb = jnp.where(causal, log_1mb, 0.0)
        hi = log_1mb.astype(bf16)
        lo = (log_1mb - hi.astype(f32)).astype(bf16)
        sums = _dot(jnp.concatenate([hi, lo], axis=0), later_ref[...])
        within = sums[:2 * T] + sums[2 * T:]
        a = jnp.exp(log_b + within + nearer)
        if diagonal:
            a = jnp.where(causal, a, 0.0)
        a = a.astype(bf16)
        a2 = jnp.concatenate([a[:T], a[T:]], axis=1)
        acc = acc + _dot(a2, stack_heads(v_ref[krows, :]))
        nearer = nearer + jnp.sum(log_1mb, axis=-1, keepdims=True)
        return nearer, acc

    def q_tile(n, carry):
        qrows = pl.ds(pl.multiple_of(n * T, T), T)
        qs = stack_heads(q_ref[qrows, :] * (1.0 / math.sqrt(HEAD_DIM)))
        state = tile(qs, n, jnp.zeros((2 * T, 1), f32), jnp.zeros((T, LANES), f32), True)
        _, acc = lax.fori_loop(0, n, lambda i, st: tile(qs, n - 1 - i, st[0], st[1], False), state)
        o_ref[qrows, :] = acc
        return carry

    lax.fori_loop(0, S // T, q_tile, 0)


def _stick_attention(hb, n_heads):
    B, S, _ = hb.shape
    nhp = n_heads // HEADS_PER_BLOCK
    T = min(STICK_TILE, S)
    blk = lambda off: pl.BlockSpec((None, S, LANES), lambda b, h, off=off: (b, 0, off + h))
    return pl.pallas_call(
        _stick_kernel,
        name="stick_attn",
        grid=(B, nhp),
        in_specs=[blk(0), blk(nhp), blk(2 * nhp)],
        out_specs=pl.BlockSpec((None, S, LANES), lambda b, h: (b, 0, h)),
        out_shape=jax.ShapeDtypeStruct((B, S, n_heads * HEAD_DIM), f32),
        scratch_shapes=[pltpu.VMEM((T, T), bf16)],
        compiler_params=_params(2),
    )(hb, hb, hb)


def _layer_norm(u, g, b):
    mu = jnp.mean(u, axis=-1, keepdims=True)
    var = jnp.mean(jnp.square(u - mu), axis=-1, keepdims=True)
    return (u - mu) * lax.rsqrt(var + LN_EPS) * g + b


def _rms_norm(o, g):
    return o * lax.rsqrt(jnp.mean(jnp.square(o), axis=-1, keepdims=True) + RMS_EPS) * g


def _mix_router_kernel(alpha, oa_ref, ob_ref, x_ref, wo_ref, ga_ref, gb_ref, lg_ref, lb_ref,
                       wr_ref, br_ref, x1_ref, idx_ref, gate_ref, pos_ref, cnt_ref, running):
    tm = x_ref.shape[0]
    wa = oa_ref.shape[1]

    @pl.when(pl.program_id(0) == 0)
    def _():
        running[...] = jnp.zeros_like(running)

    mix_a = _rms_norm(oa_ref[...], ga_ref[...]).astype(bf16)
    mix_b = _rms_norm(ob_ref[...], gb_ref[...]).astype(bf16)
    y = _dot(mix_a, wo_ref[:wa, :]) + _dot(mix_b, wo_ref[wa:, :])
    x1 = _layer_norm(alpha * x_ref[...] + y, lg_ref[...], lb_ref[...])
    x1_ref[...] = x1

    logits = jnp.dot(x1, wr_ref[...], preferred_element_type=f32,
                     precision=lax.Precision.HIGHEST) + br_ref[...]
    lane = lax.broadcasted_iota(jnp.int32, (tm, N_EXPERTS), 1)
    work = logits
    vals, sels, idxs = [], [], []
    for _ in range(TOP_K):
        m = jnp.max(work, axis=-1, keepdims=True)
        idx = jnp.min(jnp.where(work == m, lane, N_EXPERTS), axis=-1, keepdims=True)
        sel = lane == idx
        work = jnp.where(sel, -jnp.inf, work)
        vals.append(m)
        sels.append(sel)
        idxs.append(idx)
    exps = [jnp.exp(v - vals[0]) for v in vals]
    denom = functools.reduce(jnp.add, exps)

    chosen = functools.reduce(jnp.logical_or, sels)
    trow = lax.broadcasted_iota(jnp.int32, (tm, tm), 0)
    tcol = lax.broadcasted_iota(jnp.int32, (tm, tm), 1)
    earlier = (tcol < trow).astype(bf16)
    rank = _dot(earlier, chosen.astype(bf16)) + running[...]
    running[...] = running[...] + jnp.sum(chosen.astype(f32), axis=0, keepdims=True)
    cnt_ref[...] = running[...].astype(jnp.int32)

    out_lane = lax.broadcasted_iota(jnp.int32, (tm, LANES), 1)
    idx_out = jnp.zeros((tm, LANES), jnp.int32)
    gate_out = jnp.zeros((tm, LANES), f32)
    pos_out = jnp.zeros((tm, LANES), jnp.int32)
    for k in range(TOP_K):
        pos = jnp.sum(jnp.where(sels[k], rank, 0.0), axis=-1, keepdims=True).astype(jnp.int32)
        idx_out = jnp.where(out_lane == k, idxs[k], idx_out)
        gate_out = jnp.where(out_lane == k, exps[k] / denom, gate_out)
        pos_out = jnp.where(out_lane == k, pos, pos_out)
    idx_ref[...] = idx_out
    gate_ref[...] = gate_out
    pos_ref[...] = pos_out


def _mix_router(oa, ob, x, wo, layer, ga, gb, lg, lb, wr, br, alpha):
    T, D = x.shape
    W = oa.shape[1]
    tm = min(ROUTER_ROWS, T)
    rows = lambda n: pl.BlockSpec((tm, n), lambda i: (i, 0))
    full = lambda a: pl.BlockSpec(a.shape, lambda i: (0,) * a.ndim)
    return pl.pallas_call(
        functools.partial(_mix_router_kernel, alpha),
        name="mix_router",
        grid=(T // tm,),
        in_specs=[rows(W), rows(W), rows(D), pl.BlockSpec((None, D, D), lambda i: (layer, 0, 0)),
                  full(ga), full(gb), full(lg), full(lb), full(wr), full(br)],
        out_specs=[rows(D), rows(LANES), rows(LANES), rows(LANES),
                   pl.BlockSpec((1, N_EXPERTS), lambda i: (0, 0))],
        out_shape=[jax.ShapeDtypeStruct((T, D), f32),
                   jax.ShapeDtypeStruct((T, LANES), jnp.int32),
                   jax.ShapeDtypeStruct((T, LANES), f32),
                   jax.ShapeDtypeStruct((T, LANES), jnp.int32),
                   jax.ShapeDtypeStruct((1, N_EXPERTS), jnp.int32)],
        scratch_shapes=[pltpu.VMEM((1, N_EXPERTS), f32)],
        compiler_params=_params(1),
    )(oa, ob, x, wo, ga, gb, lg, lb, wr, br)


def _dispatch_kernel(dest_ref, pad_start_ref, pad_len_ref, n_used_ref, x_ref, buf_ref,
                     zeros_ref, sem, zero_sem):
    tm = x_ref.shape[0]
    i = pl.program_id(0)
    base = i * (tm * TOP_K)
    n_blocks = buf_ref.shape[0] // EXPERT_ROWS

    @pl.when(i == 0)
    def _():
        zeros_ref[...] = jnp.zeros_like(zeros_ref)

        def pad_row(e, j):
            return pltpu.make_async_copy(
                zeros_ref.at[pl.ds(0, 1), :], buf_ref.at[pl.ds(pad_start_ref[e] + j, 1), :], zero_sem)

        def unused_block(b):
            rows = pl.ds(pl.multiple_of(b * EXPERT_ROWS, EXPERT_ROWS), EXPERT_ROWS)
            return pltpu.make_async_copy(zeros_ref, buf_ref.at[rows, :], zero_sem)

        def for_each(action):
            def per_expert(e, carry):
                def per_row(j, c):
                    action(pad_row(e, j))
                    return c
                return lax.fori_loop(0, pad_len_ref[e], per_row, carry)
            lax.fori_loop(0, N_EXPERTS, per_expert, 0)

            def per_block(b, carry):
                action(unused_block(b))
                return carry
            lax.fori_loop(n_used_ref[0], n_blocks, per_block, 0)

        for_each(lambda copy: copy.start())
        for_each(lambda copy: copy.wait())

    def start(t, carry):
        for k in range(TOP_K):
            pltpu.make_async_copy(
                x_ref.at[pl.ds(t, 1), :],
                buf_ref.at[pl.ds(dest_ref[base + t * TOP_K + k], 1), :], sem).start()
        return carry

    lax.fori_loop(0, tm, start, 0, unroll=4)
    all_rows = buf_ref.at[pl.ds(0, tm * TOP_K), :]
    pltpu.make_async_copy(all_rows, all_rows, sem).wait()


def _dispatch(x1, dest_flat, pad_start, pad_len, n_used, n_rows):
    T, D = x1.shape
    tm = min(DISPATCH_ROWS, T)
    return pl.pallas_call(
        _dispatch_kernel,
        name="moe_dispatch",
        grid_spec=pltpu.PrefetchScalarGridSpec(
            num_scalar_prefetch=4,
            grid=(T // tm,),
            in_specs=[pl.BlockSpec((tm, D), lambda i, *_: (i, 0))],
            out_specs=pl.BlockSpec(memory_space=pl.ANY),
            scratch_shapes=[pltpu.VMEM((EXPERT_ROWS, D), f32),
                            pltpu.SemaphoreType.DMA(()), pltpu.SemaphoreType.DMA(())]),
        out_shape=jax.ShapeDtypeStruct((n_rows, D), f32),
        compiler_params=_params(1),
    )(dest_flat, pad_start, pad_len, n_used, x1)


def _expert_kernel(blk_expert_ref, n_used_ref, x_ref, wgu_ref, bgu_ref, wd_ref, bd_ref, y_ref):
    del blk_expert_ref
    d_ff = wd_ref.shape[0]

    @pl.when(pl.program_id(0) < n_used_ref[0])
    def _():
        h = _dot(x_ref[...].astype(bf16), wgu_ref[...]) + bgu_ref[...]
        gate = jnp.minimum(h[:, :d_ff], SWIGLU_LIMIT)
        up = jnp.clip(h[:, d_ff:], -SWIGLU_LIMIT, SWIGLU_LIMIT)
        act = (up + 1.0) * (gate * jax.nn.sigmoid(gate * SWIGLU_ALPHA))
        y_ref[...] = _dot(act.astype(bf16), wd_ref[...]) + bd_ref[...]

    @pl.when(pl.program_id(0) >= n_used_ref[0])
    def _():
        y_ref[...] = jnp.zeros_like(y_ref)


def _expert_ffn(x_buf, blk_expert, n_used, wgu, bgu, wd, bd, layer):
    R, D = x_buf.shape
    L, E, _, F2 = wgu.shape
    F = wd.shape[2]
    tm = EXPERT_ROWS
    used = lambda i, be, nu: jnp.minimum(i, nu[0] - 1)
    expert = lambda i, be, nu: (layer, be[i], 0, 0)
    return pl.pallas_call(
        _expert_kernel,
        name="expert_ffn",
        grid_spec=pltpu.PrefetchScalarGridSpec(
            num_scalar_prefetch=2,
            grid=(R // tm,),
            in_specs=[pl.BlockSpec((tm, D), lambda i, be, nu: (used(i, be, nu), 0)),
                      pl.BlockSpec((None, None, D, F2), expert),
                      pl.BlockSpec((None, None, 1, F2), expert),
                      pl.BlockSpec((None, None, F, D), expert),
                      pl.BlockSpec((None, None, 1, D), expert)],
            out_specs=pl.BlockSpec((tm, D), lambda i, be, nu: (i, 0))),
        out_shape=jax.ShapeDtypeStruct((R, D), f32),
        compiler_params=_params(1),
    )(blk_expert, n_used, x_buf, wgu, bgu.reshape(L, E, 1, F2), wd, bd.reshape(L, E, 1, D))


def _combine_kernel(alpha, dest_ref, y_hbm, x_ref, gate_ref, lg_ref, lb_ref, o_ref, rows_buf, sems):
    tm = x_ref.shape[0]
    i = pl.program_id(0)
    n = pl.num_programs(0)

    def start_tile(tile, slot):
        def body(t, carry):
            for k in range(TOP_K):
                pltpu.make_async_copy(
                    y_hbm.at[pl.ds(dest_ref[(tile * tm + t) * TOP_K + k], 1), :],
                    rows_buf.at[slot, k, pl.ds(t, 1), :], sems.at[slot]).start()
            return carry
        lax.fori_loop(0, tm, body, 0, unroll=4)

    slot = i % 2

    @pl.when(i == 0)
    def _():
        start_tile(0, 0)

    @pl.when(i + 1 < n)
    def _():
        start_tile(i + 1, 1 - slot)

    for k in range(TOP_K):
        pltpu.make_async_copy(y_hbm.at[pl.ds(0, tm), :], rows_buf.at[slot, k], sems.at[slot]).wait()

    y = jnp.zeros(x_ref.shape, f32)
    for k in range(TOP_K):
        y = y + gate_ref[:, k:k + 1] * rows_buf[slot, k]
    o_ref[...] = _layer_norm(alpha * x_ref[...] + y, lg_ref[...], lb_ref[...])


def _combine(y_buf, dest_flat, x1, gates, lg, lb, alpha):
    T, D = x1.shape
    tm = min(COMBINE_ROWS, T)
    return pl.pallas_call(
        functools.partial(_combine_kernel, alpha),
        name="moe_combine",
        grid_spec=pltpu.PrefetchScalarGridSpec(
            num_scalar_prefetch=1,
            grid=(T // tm,),
            in_specs=[pl.BlockSpec(memory_space=pl.ANY),
                      pl.BlockSpec((tm, D), lambda i, d: (i, 0)),
                      pl.BlockSpec((tm, LANES), lambda i, d: (i, 0)),
                      pl.BlockSpec((1, D), lambda i, d: (0, 0)),
                      pl.BlockSpec((1, D), lambda i, d: (0, 0))],
            out_specs=pl.BlockSpec((tm, D), lambda i, d: (i, 0)),
            scratch_shapes=[pltpu.VMEM((2, TOP_K, tm, D), f32),
                            pltpu.SemaphoreType.DMA((2,))]),
        out_shape=jax.ShapeDtypeStruct((T, D), f32),
        compiler_params=_params(1),
    )(dest_flat, y_buf, x1, gates, lg, lb)


def _moe(x1, idx, gates, pos, counts, wgu, bgu, wd, bd, layer, lg, lb, alpha):
    T, D = x1.shape
    n_assign = T * TOP_K
    counts = counts.reshape(N_EXPERTS)
    padded = (counts + EXPERT_ROWS - 1) // EXPERT_ROWS * EXPERT_ROWS
    group_end = jnp.cumsum(padded)
    group_start = group_end - padded
    experts = jnp.arange(N_EXPERTS, dtype=jnp.int32)
    start_of = jnp.sum(jnp.where(idx[:, :TOP_K, None] == experts, group_start, 0), axis=-1)
    dest = (start_of + pos[:, :TOP_K]).reshape(n_assign).astype(jnp.int32)
    n_blocks = -(-n_assign // EXPERT_ROWS) + N_EXPERTS
    block_row = jnp.arange(n_blocks, dtype=jnp.int32)[:, None] * EXPERT_ROWS
    blk_expert = jnp.minimum(jnp.sum(group_end[None, :] <= block_row, axis=-1),
                             N_EXPERTS - 1).astype(jnp.int32)
    n_used = (group_end[-1:] // EXPERT_ROWS).astype(jnp.int32)
    x_buf = _dispatch(x1, dest, (group_start + counts).astype(jnp.int32),
                      (padded - counts).astype(jnp.int32), n_used, n_blocks * EXPERT_ROWS)
    y_buf = _expert_ffn(x_buf, blk_expert, n_used, wgu, bgu, wd, bd, layer)
    return _combine(y_buf, dest, x1, gates, lg, lb, alpha)


def kernel(x, w_in, g_mix_a, g_mix_b, w_out, ln1_g, ln1_b, w_router, b_router,
           w_gate_up, b_gate_up, w_down, b_down, ln2_g, ln2_b):
    B, S, D = x.shape
    depth = w_in.shape[0]
    wa = g_mix_a.shape[1]
    wb = g_mix_b.shape[1]
    heads_a = wa // HEAD_DIM
    heads_b = wb // HEAD_DIM
    alpha = (2.0 * depth) ** 0.25
    slopes = jnp.asarray([2.0 ** (-8.0 * (h + 1) / heads_a) for h in range(heads_a)], f32)
    row = lambda a: a.reshape(1, -1)

    assert wa == wb, "the q/k/v projection is split into two equal column halves"
    w_in_b = w_in.astype(bf16)
    w_out_b = w_out.astype(bf16)
    w_gate_up_b = w_gate_up.astype(bf16)
    w_down_b = w_down.astype(bf16)

    xt = x.reshape(B * S, D)
    for l in range(depth):
        ha = _proj(xt, w_in_b, l, 0, f32).reshape(B, S, 3 * wa)
        hb = _proj(xt, w_in_b, l, 1, bf16).reshape(B, S, 3 * wb)
        oa = _dilated_attention(ha, slopes, heads_a).reshape(B * S, wa)
        ob = _stick_attention(hb, heads_b).reshape(B * S, wb)
        x1, idx, gates, pos, counts = _mix_router(
            oa, ob, xt, w_out_b, l, row(g_mix_a[l]), row(g_mix_b[l]),
            row(ln1_g[l]), row(ln1_b[l]), w_router[l], row(b_router[l]), alpha)
        xt = _moe(x1, idx, gates, pos, counts, w_gate_up_b, b_gate_up, w_down_b, b_down, l,
                  row(ln2_g[l]), row(ln2_b[l]), alpha)
    return xt.reshape(B, S, D)
```

```python
import functools
import math

import jax
import jax.numpy as jnp
from jax import lax
from jax.experimental import pallas as pl
from jax.experimental.pallas import tpu as pltpu

HEAD_DIM = 64
HEADS_PER_BLOCK = 2
LANES = HEADS_PER_BLOCK * HEAD_DIM
Q_BLOCK = 128
DILATED_PATTERNS = ((128, 1), (512, 4), (2048, 16))
N_EXPERTS = 32
TOP_K = 4
SWIGLU_LIMIT = 7.0
SWIGLU_ALPHA = 1.702
LN_EPS = 1e-5
RMS_EPS = 1e-6
MASKED = -1e30

VMEM_LIMIT = 56 * 1024 * 1024
PROJ_ROWS = 256
ROUTER_ROWS = 256
DISPATCH_ROWS = 128
EXPERT_ROWS = 256
COMBINE_ROWS = 128
STICK_TILE = 256
STICK_UNDERFLOW = -120.0

f32 = jnp.float32
bf16 = jnp.bfloat16


def _params(n_grid_dims):
    return pltpu.CompilerParams(
        dimension_semantics=("arbitrary",) * n_grid_dims, vmem_limit_bytes=VMEM_LIMIT)


def _dot_nt(a, b):
    return lax.dot_general(a, b, (((1,), (1,)), ((), ())), preferred_element_type=f32)


def _dot(a, b):
    return jnp.dot(a, b, preferred_element_type=f32)


def _proj_kernel(x_ref, w_ref, o_ref):
    o_ref[...] = _dot(x_ref[...].astype(bf16), w_ref[...]).astype(o_ref.dtype)


def _proj(x, w, layer, half, out_dtype):
    T, K = x.shape
    N = w.shape[2] // 2
    tm = min(PROJ_ROWS, T)
    return pl.pallas_call(
        _proj_kernel,
        name="proj",
        grid=(T // tm,),
        in_specs=[pl.BlockSpec((tm, K), lambda i: (i, 0)),
                  pl.BlockSpec((None, K, N), lambda i: (layer, 0, half))],
        out_specs=pl.BlockSpec((tm, N), lambda i: (i, 0)),
        out_shape=jax.ShapeDtypeStruct((T, N), out_dtype),
        compiler_params=_params(1),
    )(x, w)


def _dilated_kernel(slopes_ref, q_ref, k_ref, v_ref, o_ref, acc_o, acc_m, acc_l, bias_ref,
                    s0_ref, s1_ref):
    S = q_ref.shape[0]
    QB = Q_BLOCK
    hp = pl.program_id(1)
    head0 = lax.broadcasted_iota(jnp.int32, (QB, LANES), 1) < HEAD_DIM

    def stack_heads(x):
        first = lax.broadcasted_iota(jnp.int32, x.shape, 1) < HEAD_DIM
        return jnp.concatenate([jnp.where(first, x, jnp.zeros_like(x)),
                                jnp.where(first, jnp.zeros_like(x), x)], axis=0)

    for r, (window, d) in enumerate(DILATED_PATTERNS):
        assert window // d == QB
        nblk = (S // d) // QB
        nk = 2 * QB if nblk > 1 else QB
        srow = lax.broadcasted_iota(jnp.int32, (2 * QB, nk), 0)
        scol = lax.broadcasted_iota(jnp.int32, (2 * QB, nk), 1)
        second = srow >= QB
        coef = jnp.where(second, -(slopes_ref[hp * HEADS_PER_BLOCK + 1] * float(d)),
                         -(slopes_ref[hp * HEADS_PER_BLOCK] * float(d)))
        for variant, offset in enumerate((nk - QB, 0)):
            delta = jnp.where(second, srow - QB, srow) - scol + offset
            ok = jnp.logical_and(delta >= 0, delta <= QB)
            bias_ref[variant, :, :nk] = jnp.where(ok, coef * delta.astype(f32), MASKED)

        n_steps = d * nblk

        def block_rows(idx, d=d, nblk=nblk, nk=nk):
            c = idx // nblk
            nb = idx % nblk
            strided = lambda start, n: pl.ds(start, n, stride=d) if d > 1 else pl.ds(start, n)
            rows = strided(nb * (QB * d) + c, QB)
            krows = strided(jnp.maximum(nb - (nk // QB - 1), 0) * (QB * d) + c, nk)
            variant = jnp.where(nb > 0, 0, 1) if nblk > 1 else 0
            return rows, krows, variant

        def scores(idx, s_ref, nk=nk):
            rows, krows, variant = block_rows(idx)
            qs = stack_heads((q_ref[rows, :] * (1.0 / math.sqrt(HEAD_DIM))).astype(bf16))
            s = _dot_nt(qs, k_ref[krows, :].astype(bf16)) + bias_ref[variant, :, :nk]
            s_ref[:, :nk] = s

        def softmax_pv(idx, s_ref, r=r, nk=nk):
            rows, krows, _ = block_rows(idx)
            s = s_ref[:, :nk]
            m = jnp.max(s, axis=-1, keepdims=True)
            p = jnp.exp(s - m)
            l = jnp.sum(p, axis=-1, keepdims=True)
            p = p.astype(bf16)
            p2 = jnp.concatenate([p[:QB], p[QB:]], axis=1)
            acc_o[r, rows, :] = _dot(p2, stack_heads(v_ref[krows, :].astype(bf16)))
            acc_m[r, rows, :] = jnp.where(head0, m[:QB], m[QB:])
            acc_l[r, rows, :] = jnp.where(head0, l[:QB], l[QB:])

        assert n_steps % 2 == 0
        scores(0, s0_ref)

        def step_pair(j, carry, n_steps=n_steps):
            scores(2 * j + 1, s1_ref)
            softmax_pv(2 * j, s0_ref)
            scores(jnp.minimum(2 * j + 2, n_steps - 1), s0_ref)
            softmax_pv(2 * j + 1, s1_ref)
            return carry

        lax.fori_loop(0, n_steps // 2, step_pair, 0)

    def merge(i, carry):
        rows = pl.ds(pl.multiple_of(i * Q_BLOCK, Q_BLOCK), Q_BLOCK)
        ms = [acc_m[r, rows, :] for r in range(len(DILATED_PATTERNS))]
        m_all = functools.reduce(jnp.maximum, ms)
        num = jnp.zeros((Q_BLOCK, LANES), f32)
        den = jnp.zeros((Q_BLOCK, LANES), f32)
        for r in range(len(DILATED_PATTERNS)):
            w = jnp.exp(ms[r] - m_all)
            num = num + w * acc_o[r, rows, :]
            den = den + w * acc_l[r, rows, :]
        o_ref[rows, :] = num / den
        return carry

    lax.fori_loop(0, S // Q_BLOCK, merge, 0)


def _dilated_attention(ha, slopes, n_heads):
    B, S, _ = ha.shape
    nhp = n_heads // HEADS_PER_BLOCK
    blk = lambda off: pl.BlockSpec((None, S, LANES), lambda b, h, sl, off=off: (b, 0, off + h))
    return pl.pallas_call(
        _dilated_kernel,
        name="dilated_attn",
        grid_spec=pltpu.PrefetchScalarGridSpec(
            num_scalar_prefetch=1,
            grid=(B, nhp),
            in_specs=[blk(0), blk(nhp), blk(2 * nhp)],
            out_specs=pl.BlockSpec((None, S, LANES), lambda b, h, sl: (b, 0, h)),
            scratch_shapes=[pltpu.VMEM((len(DILATED_PATTERNS), S, LANES), f32)] * 3
            + [pltpu.VMEM((2, 2 * Q_BLOCK, 2 * Q_BLOCK), f32)]
            + [pltpu.VMEM((2 * Q_BLOCK, 2 * Q_BLOCK), f32)] * 2),
        out_shape=jax.ShapeDtypeStruct((B, S, n_heads * HEAD_DIM), f32),
        compiler_params=_params(2),
    )(slopes, ha, ha, ha)


def _stick_kernel(q_ref, k_ref, v_ref, o_ref, later_ref):
    S = q_ref.shape[0]
    T = min(STICK_TILE, S)
    lane = lax.broadcasted_iota(jnp.int32, (T, LANES), 1)
    head0 = lane < HEAD_DIM
    krow = lax.broadcasted_iota(jnp.int32, (T, T), 0)
    kcol = lax.broadcasted_iota(jnp.int32, (T, T), 1)
    later_ref[...] = (krow > kcol).astype(bf16)
    srow = lax.broadcasted_iota(jnp.int32, (2 * T, T), 0)
    scol = lax.broadcasted_iota(jnp.int32, (2 * T, T), 1)
    causal = scol < jnp.where(srow >= T, srow - T, srow)

    def stack_heads(x):
        return jnp.concatenate([jnp.where(head0, x, jnp.zeros_like(x)),
                                jnp.where(head0, jnp.zeros_like(x), x)], axis=0)

    def tile(qs, kb, nearer, acc, diagonal):
        krows = pl.ds(pl.multiple_of(kb * T, T), T)
        z = _dot_nt(qs, k_ref[krows, :])
        log_b = jnp.minimum(z, 0.0) - jnp.log(1.0 + jnp.exp(-jnp.abs(z)))
        log_1mb = log_b - z
        if diagonal:
            log_1mb = jnp.where(causal, log_1mb, 0.0)
        hi = log_1mb.astype(bf16)
        lo = (log_1mb - hi.astype(f32)).astype(bf16)
        sums = _dot(jnp.concatenate([hi, lo], axis=0), later_ref[...])
        within = sums[:2 * T] + sums[2 * T:]
        a = jnp.exp(log_b + within + nearer)
        if diagonal:
            a = jnp.where(causal, a, 0.0)
        a = a.astype(bf16)
        a2 = jnp.concatenate([a[:T], a[T:]], axis=1)
        acc = acc + _dot(a2, stack_heads(v_ref[krows, :]))
        nearer = nearer + jnp.sum(log_1mb, axis=-1, keepdims=True)
        return nearer, acc

    def q_tile(n, carry):
        qrows = pl.ds(pl.multiple_of(n * T, T), T)
        qs = stack_heads(q_ref[qrows, :] * (1.0 / math.sqrt(HEAD_DIM)))
        nearer, acc = tile(qs, n, jnp.zeros((2 * T, 1), f32), jnp.zeros((T, LANES), f32), True)

        def more(state):
            i, nearer, _ = state
            return jnp.logical_and(i < n, jnp.max(nearer) > STICK_UNDERFLOW)

        def farther(state):
            i, nearer, acc = state
            nearer, acc = tile(qs, n - 1 - i, nearer, acc, False)
            return i + 1, nearer, acc

        _, _, acc = lax.while_loop(more, farther, (jnp.int32(0), nearer, acc))
        o_ref[qrows, :] = acc
        return carry

    lax.fori_loop(0, S // T, q_tile, 0)


def _stick_attention(hb, n_heads):
    B, S, _ = hb.shape
    nhp = n_heads // HEADS_PER_BLOCK
    T = min(STICK_TILE, S)
    blk = lambda off: pl.BlockSpec((None, S, LANES), lambda b, h, off=off: (b, 0, off + h))
    return pl.pallas_call(
        _stick_kernel,
        name="stick_attn",
        grid=(B, nhp),
        in_specs=[blk(0), blk(nhp), blk(2 * nhp)],
        out_specs=pl.BlockSpec((None, S, LANES), lambda b, h: (b, 0, h)),
        out_shape=jax.ShapeDtypeStruct((B, S, n_heads * HEAD_DIM), f32),
        scratch_shapes=[pltpu.VMEM((T, T), bf16)],
        compiler_params=_params(2),
    )(hb, hb, hb)


def _layer_norm(u, g, b):
    mu = jnp.mean(u, axis=-1, keepdims=True)
    var = jnp.mean(jnp.square(u - mu), axis=-1, keepdims=True)
    return (u - mu) * lax.rsqrt(var + LN_EPS) * g + b


def _rms_norm(o, g):
    return o * lax.rsqrt(jnp.mean(jnp.square(o), axis=-1, keepdims=True) + RMS_EPS) * g


def _mix_router_kernel(alpha, oa_ref, ob_ref, x_ref, wo_ref, ga_ref, gb_ref, lg_ref, lb_ref,
                       wr_ref, br_ref, x1_ref, idx_ref, gate_ref, pos_ref, cnt_ref, running):
    tm = x_ref.shape[0]
    wa = oa_ref.shape[1]

    @pl.when(pl.program_id(0) == 0)
    def _():
        running[...] = jnp.zeros_like(running)

    mix_a = _rms_norm(oa_ref[...], ga_ref[...]).astype(bf16)
    mix_b = _rms_norm(ob_ref[...], gb_ref[...]).astype(bf16)
    y = _dot(mix_a, wo_ref[:wa, :]) + _dot(mix_b, wo_ref[wa:, :])
    x1 = _layer_norm(alpha * x_ref[...] + y, lg_ref[...], lb_ref[...])
    x1_ref[...] = x1

    logits = jnp.dot(x1, wr_ref[...], preferred_element_type=f32,
                     precision=lax.Precision.HIGHEST) + br_ref[...]
    lane = lax.broadcasted_iota(jnp.int32, (tm, N_EXPERTS), 1)
    work = logits
    vals, sels, idxs = [], [], []
    for _ in range(TOP_K):
        m = jnp.max(work, axis=-1, keepdims=True)
        idx = jnp.min(jnp.where(work == m, lane, N_EXPERTS), axis=-1, keepdims=True)
        sel = lane == idx
        work = jnp.where(sel, -jnp.inf, work)
        vals.append(m)
        sels.append(sel)
        idxs.append(idx)
    exps = [jnp.exp(v - vals[0]) for v in vals]
    denom = functools.reduce(jnp.add, exps)

    chosen = functools.reduce(jnp.logical_or, sels)
    trow = lax.broadcasted_iota(jnp.int32, (tm, tm), 0)
    tcol = lax.broadcasted_iota(jnp.int32, (tm, tm), 1)
    earlier = (tcol < trow).astype(bf16)
    rank = _dot(earlier, chosen.astype(bf16)) + running[...]
    running[...] = running[...] + jnp.sum(chosen.astype(f32), axis=0, keepdims=True)
    cnt_ref[...] = running[...].astype(jnp.int32)

    out_lane = lax.broadcasted_iota(jnp.int32, (tm, LANES), 1)
    idx_out = jnp.zeros((tm, LANES), jnp.int32)
    gate_out = jnp.zeros((tm, LANES), f32)
    pos_out = jnp.zeros((tm, LANES), jnp.int32)
    for k in range(TOP_K):
        pos = jnp.sum(jnp.where(sels[k], rank, 0.0), axis=-1, keepdims=True).astype(jnp.int32)
        idx_out = jnp.where(out_lane == k, idxs[k], idx_out)
        gate_out = jnp.where(out_lane == k, exps[k] / denom, gate_out)
        pos_out = jnp.where(out_lane == k, pos, pos_out)
    idx_ref[...] = idx_out
    gate_ref[...] = gate_out
    pos_ref[...] = pos_out


def _mix_router(oa, ob, x, wo, layer, ga, gb, lg, lb, wr, br, alpha):
    T, D = x.shape
    W = oa.shape[1]
    tm = min(ROUTER_ROWS, T)
    rows = lambda n: pl.BlockSpec((tm, n), lambda i: (i, 0))
    full = lambda a: pl.BlockSpec(a.shape, lambda i: (0,) * a.ndim)
    return pl.pallas_call(
        functools.partial(_mix_router_kernel, alpha),
        name="mix_router",
        grid=(T // tm,),
        in_specs=[rows(W), rows(W), rows(D), pl.BlockSpec((None, D, D), lambda i: (layer, 0, 0)),
                  full(ga), full(gb), full(lg), full(lb), full(wr), full(br)],
        out_specs=[rows(D), rows(LANES), rows(LANES), rows(LANES),
                   pl.BlockSpec((1, N_EXPERTS), lambda i: (0, 0))],
        out_shape=[jax.ShapeDtypeStruct((T, D), f32),
                   jax.ShapeDtypeStruct((T, LANES), jnp.int32),
                   jax.ShapeDtypeStruct((T, LANES), f32),
                   jax.ShapeDtypeStruct((T, LANES), jnp.int32),
                   jax.ShapeDtypeStruct((1, N_EXPERTS), jnp.int32)],
        scratch_shapes=[pltpu.VMEM((1, N_EXPERTS), f32)],
        compiler_params=_params(1),
    )(oa, ob, x, wo, ga, gb, lg, lb, wr, br)


def _dispatch_kernel(dest_ref, pad_start_ref, pad_len_ref, n_used_ref, x_ref, buf_ref,
                     zeros_ref, sem, zero_sem):
    tm = x_ref.shape[0]
    i = pl.program_id(0)
    base = i * (tm * TOP_K)
    n_blocks = buf_ref.shape[0] // EXPERT_ROWS

    @pl.when(i == 0)
    def _():
        zeros_ref[...] = jnp.zeros_like(zeros_ref)

        def pad_row(e, j):
            return pltpu.make_async_copy(
                zeros_ref.at[pl.ds(0, 1), :], buf_ref.at[pl.ds(pad_start_ref[e] + j, 1), :], zero_sem)

        def unused_block(b):
            rows = pl.ds(pl.multiple_of(b * EXPERT_ROWS, EXPERT_ROWS), EXPERT_ROWS)
            return pltpu.make_async_copy(zeros_ref, buf_ref.at[rows, :], zero_sem)

        def for_each(action):
            def per_expert(e, carry):
                def per_row(j, c):
                    action(pad_row(e, j))
                    return c
                return lax.fori_loop(0, pad_len_ref[e], per_row, carry)
            lax.fori_loop(0, N_EXPERTS, per_expert, 0)

            def per_block(b, carry):
                action(unused_block(b))
                return carry
            lax.fori_loop(n_used_ref[0], n_blocks, per_block, 0)

        for_each(lambda copy: copy.start())
        for_each(lambda copy: copy.wait())

    def start(t, carry):
        for k in range(TOP_K):
            pltpu.make_async_copy(
                x_ref.at[pl.ds(t, 1), :],
                buf_ref.at[pl.ds(dest_ref[base + t * TOP_K + k], 1), :], sem).start()
        return carry

    lax.fori_loop(0, tm, start, 0, unroll=4)
    all_rows = buf_ref.at[pl.ds(0, tm * TOP_K), :]
    pltpu.make_async_copy(all_rows, all_rows, sem).wait()


def _dispatch(x1, dest_flat, pad_start, pad_len, n_used, n_rows):
    T, D = x1.shape
    tm = min(DISPATCH_ROWS, T)
    return pl.pallas_call(
        _dispatch_kernel,
        name="moe_dispatch",
        grid_spec=pltpu.PrefetchScalarGridSpec(
            num_scalar_prefetch=4,
            grid=(T // tm,),
            in_specs=[pl.BlockSpec((tm, D), lambda i, *_: (i, 0))],
            out_specs=pl.BlockSpec(memory_space=pl.ANY),
            scratch_shapes=[pltpu.VMEM((EXPERT_ROWS, D), f32),
                            pltpu.SemaphoreType.DMA(()), pltpu.SemaphoreType.DMA(())]),
        out_shape=jax.ShapeDtypeStruct((n_rows, D), f32),
        compiler_params=_params(1),
    )(dest_flat, pad_start, pad_len, n_used, x1)


def _expert_kernel(blk_expert_ref, n_used_ref, x_ref, wgu_ref, bgu_ref, wd_ref, bd_ref, y_ref):
    del blk_expert_ref
    d_ff = wd_ref.shape[0]

    @pl.when(pl.program_id(0) < n_used_ref[0])
    def _():
        h = _dot(x_ref[...].astype(bf16), wgu_ref[...]) + bgu_ref[...]
        gate = jnp.minimum(h[:, :d_ff], SWIGLU_LIMIT)
        up = jnp.clip(h[:, d_ff:], -SWIGLU_LIMIT, SWIGLU_LIMIT)
        act = (up + 1.0) * (gate * jax.nn.sigmoid(gate * SWIGLU_ALPHA))
        y_ref[...] = _dot(act.astype(bf16), wd_ref[...]) + bd_ref[...]

    @pl.when(pl.program_id(0) >= n_used_ref[0])
    def _():
        y_ref[...] = jnp.zeros_like(y_ref)


def _expert_ffn(x_buf, blk_expert, n_used, wgu, bgu, wd, bd, layer):
    R, D = x_buf.shape
    L, E, _, F2 = wgu.shape
    F = wd.shape[2]
    tm = EXPERT_ROWS
    used = lambda i, be, nu: jnp.minimum(i, nu[0] - 1)
    expert = lambda i, be, nu: (layer, be[i], 0, 0)
    return pl.pallas_call(
        _expert_kernel,
        name="expert_ffn",
        grid_spec=pltpu.PrefetchScalarGridSpec(
            num_scalar_prefetch=2,
            grid=(R // tm,),
            in_specs=[pl.BlockSpec((tm, D), lambda i, be, nu: (used(i, be, nu), 0)),
                      pl.BlockSpec((None, None, D, F2), expert),
                      pl.BlockSpec((None, None, 1, F2), expert),
                      pl.BlockSpec((None, None, F, D), expert),
                      pl.BlockSpec((None, None, 1, D), expert)],
            out_specs=pl.BlockSpec((tm, D), lambda i, be, nu: (i, 0))),
        out_shape=jax.ShapeDtypeStruct((R, D), f32),
        compiler_params=_params(1),
    )(blk_expert, n_used, x_buf, wgu, bgu.reshape(L, E, 1, F2), wd, bd.reshape(L, E, 1, D))


def _combine_kernel(alpha, dest_ref, y_hbm, x_ref, gate_ref, lg_ref, lb_ref, o_ref, rows_buf, sems):
    tm = x_ref.shape[0]
    i = pl.program_id(0)
    n = pl.num_programs(0)

    def start_tile(tile, slot):
        def body(t, carry):
            for k in range(TOP_K):
                pltpu.make_async_copy(
                    y_hbm.at[pl.ds(dest_ref[(tile * tm + t) * TOP_K + k], 1), :],
                    rows_buf.at[slot, k, pl.ds(t, 1), :], sems.at[slot]).start()
            return carry
        lax.fori_loop(0, tm, body, 0, unroll=4)

    slot = i % 2

    @pl.when(i == 0)
    def _():
        start_tile(0, 0)

    @pl.when(i + 1 < n)
    def _():
        start_tile(i + 1, 1 - slot)

    for k in range(TOP_K):
        pltpu.make_async_copy(y_hbm.at[pl.ds(0, tm), :], rows_buf.at[slot, k], sems.at[slot]).wait()

    y = jnp.zeros(x_ref.shape, f32)
    for k in range(TOP_K):
        y = y + gate_ref[:, k:k + 1] * rows_buf[slot, k]
    o_ref[...] = _layer_norm(alpha * x_ref[...] + y, lg_ref[...], lb_ref[...])


def _combine(y_buf, dest_flat, x1, gates, lg, lb, alpha):
    T, D = x1.shape
    tm = min(COMBINE_ROWS, T)
    return pl.pallas_call(
        functools.partial(_combine_kernel, alpha),
        name="moe_combine",
        grid_spec=pltpu.PrefetchScalarGridSpec(
            num_scalar_prefetch=1,
            grid=(T // tm,),
            in_specs=[pl.BlockSpec(memory_space=pl.ANY),
                      pl.BlockSpec((tm, D), lambda i, d: (i, 0)),
                      pl.BlockSpec((tm, LANES), lambda i, d: (i, 0)),
                      pl.BlockSpec((1, D), lambda i, d: (0, 0)),
                      pl.BlockSpec((1, D), lambda i, d: (0, 0))],
            out_specs=pl.BlockSpec((tm, D), lambda i, d: (i, 0)),
            scratch_shapes=[pltpu.VMEM((2, TOP_K, tm, D), f32),
                            pltpu.SemaphoreType.DMA((2,))]),
        out_shape=jax.ShapeDtypeStruct((T, D), f32),
        compiler_params=_params(1),
    )(dest_flat, y_buf, x1, gates, lg, lb)


def _moe(x1, idx, gates, pos, counts, wgu, bgu, wd, bd, layer, lg, lb, alpha):
    T, D = x1.shape
    n_assign = T * TOP_K
    counts = counts.reshape(N_EXPERTS)
    padded = (counts + EXPERT_ROWS - 1) // EXPERT_ROWS * EXPERT_ROWS
    group_end = jnp.cumsum(padded)
    group_start = group_end - padded
    experts = jnp.arange(N_EXPERTS, dtype=jnp.int32)
    start_of = jnp.sum(jnp.where(idx[:, :TOP_K, None] == experts, group_start, 0), axis=-1)
    dest = (start_of + pos[:, :TOP_K]).reshape(n_assign).astype(jnp.int32)
    n_blocks = -(-n_assign // EXPERT_ROWS) + N_EXPERTS
    block_row = jnp.arange(n_blocks, dtype=jnp.int32)[:, None] * EXPERT_ROWS
    blk_expert = jnp.minimum(jnp.sum(group_end[None, :] <= block_row, axis=-1),
                             N_EXPERTS - 1).astype(jnp.int32)
    n_used = (group_end[-1:] // EXPERT_ROWS).astype(jnp.int32)
    x_buf = _dispatch(x1, dest, (group_start + counts).astype(jnp.int32),
                      (padded - counts).astype(jnp.int32), n_used, n_blocks * EXPERT_ROWS)
    y_buf = _expert_ffn(x_buf, blk_expert, n_used, wgu, bgu, wd, bd, layer)
    return _combine(y_buf, dest, x1, gates, lg, lb, alpha)


def kernel(x, w_in, g_mix_a, g_mix_b, w_out, ln1_g, ln1_b, w_router, b_router,
           w_gate_up, b_gate_up, w_down, b_down, ln2_g, ln2_b):
    B, S, D = x.shape
    depth = w_in.shape[0]
    wa = g_mix_a.shape[1]
    wb = g_mix_b.shape[1]
    heads_a = wa // HEAD_DIM
    heads_b = wb // HEAD_DIM
    alpha = (2.0 * depth) ** 0.25
    slopes = jnp.asarray([2.0 ** (-8.0 * (h + 1) / heads_a) for h in range(heads_a)], f32)
    row = lambda a: a.reshape(1, -1)

    assert wa == wb, "the q/k/v projection is split into two equal column halves"
    w_in_b = w_in.astype(bf16)
    w_out_b = w_out.astype(bf16)
    w_gate_up_b = w_gate_up.astype(bf16)
    w_down_b = w_down.astype(bf16)

    xt = x.reshape(B * S, D)
    for l in range(depth):
        ha = _proj(xt, w_in_b, l, 0, f32).reshape(B, S, 3 * wa)
        hb = _proj(xt, w_in_b, l, 1, bf16).reshape(B, S, 3 * wb)
        oa = _dilated_attention(ha, slopes, heads_a).reshape(B * S, wa)
        ob = _stick_attention(hb, heads_b).reshape(B * S, wb)
        x1, idx, gates, pos, counts = _mix_router(
            oa, ob, xt, w_out_b, l, row(g_mix_a[l]), row(g_mix_b[l]),
            row(ln1_g[l]), row(ln1_b[l]), w_router[l], row(b_router[l]), alpha)
        xt = _moe(x1, idx, gates, pos, counts, w_gate_up_b, b_gate_up, w_down_b, b_down, l,
                  row(ln2_g[l]), row(ln2_b[l]), alpha)
    return xt.reshape(B, S, D)
```

```python
import functools
import math

import jax
import jax.numpy as jnp
from jax import lax
from jax.experimental import pallas as pl
from jax.experimental.pallas import tpu as pltpu

HEAD_DIM = 64
HEADS_PER_BLOCK = 2
LANES = HEADS_PER_BLOCK * HEAD_DIM
Q_BLOCK = 128
DILATED_PATTERNS = ((128, 1), (512, 4), (2048, 16))
N_EXPERTS = 32
TOP_K = 4
SWIGLU_LIMIT = 7.0
SWIGLU_ALPHA = 1.702
LN_EPS = 1e-5
RMS_EPS = 1e-6
MASKED = -1e30

VMEM_LIMIT = 56 * 1024 * 1024
PROJ_ROWS = 256
ROUTER_ROWS = 256
DISPATCH_ROWS = 128
EXPERT_ROWS = 256
WEIGHT_CAST_ROWS = 256
COMBINE_ROWS = 128
STICK_TILE = 256
STICK_UNDERFLOW = -120.0

f32 = jnp.float32
bf16 = jnp.bfloat16


def _params(n_grid_dims):
    return pltpu.CompilerParams(
        dimension_semantics=("arbitrary",) * n_grid_dims, vmem_limit_bytes=VMEM_LIMIT)


def _dot_nt(a, b):
    return lax.dot_general(a, b, (((1,), (1,)), ((), ())), preferred_element_type=f32)


def _dot(a, b):
    return jnp.dot(a, b, preferred_element_type=f32)


def _proj_kernel(x_ref, w_ref, o_ref):
    o_ref[...] = _dot(x_ref[...].astype(bf16), w_ref[...]).astype(o_ref.dtype)


def _proj(x, w, layer, half, out_dtype):
    T, K = x.shape
    N = w.shape[2] // 2
    tm = min(PROJ_ROWS, T)
    return pl.pallas_call(
        _proj_kernel,
        name="proj",
        grid=(T // tm,),
        in_specs=[pl.BlockSpec((tm, K), lambda i: (i, 0)),
                  pl.BlockSpec((None, K, N), lambda i: (layer, 0, half))],
        out_specs=pl.BlockSpec((tm, N), lambda i: (i, 0)),
        out_shape=jax.ShapeDtypeStruct((T, N), out_dtype),
        compiler_params=_params(1),
    )(x, w)


def _dilated_kernel(slopes_ref, q_ref, k_ref, v_ref, o_ref, acc_o, acc_m, acc_l, bias_ref,
                    s0_ref, s1_ref):
    S = q_ref.shape[0]
    QB = Q_BLOCK
    hp = pl.program_id(1)
    head0 = lax.broadcasted_iota(jnp.int32, (QB, LANES), 1) < HEAD_DIM

    def stack_heads(x):
        first = lax.broadcasted_iota(jnp.int32, x.shape, 1) < HEAD_DIM
        return jnp.concatenate([jnp.where(first, x, jnp.zeros_like(x)),
                                jnp.where(first, jnp.zeros_like(x), x)], axis=0)

    for r, (window, d) in enumerate(DILATED_PATTERNS):
        assert window // d == QB
        nblk = (S // d) // QB
        nk = 2 * QB if nblk > 1 else QB
        srow = lax.broadcasted_iota(jnp.int32, (2 * QB, nk), 0)
        scol = lax.broadcasted_iota(jnp.int32, (2 * QB, nk), 1)
        second = srow >= QB
        coef = jnp.where(second, -(slopes_ref[hp * HEADS_PER_BLOCK + 1] * float(d)),
                         -(slopes_ref[hp * HEADS_PER_BLOCK] * float(d)))
        for variant, offset in enumerate((nk - QB, 0)):
            delta = jnp.where(second, srow - QB, srow) - scol + offset
            ok = jnp.logical_and(delta >= 0, delta <= QB)
            bias_ref[variant, :, :nk] = jnp.where(ok, coef * delta.astype(f32), MASKED)

        n_steps = d * nblk

        def block_rows(idx, d=d, nblk=nblk, nk=nk):
            c = idx // nblk
            nb = idx % nblk
            strided = lambda start, n: pl.ds(start, n, stride=d) if d > 1 else pl.ds(start, n)
            rows = strided(nb * (QB * d) + c, QB)
            krows = strided(jnp.maximum(nb - (nk // QB - 1), 0) * (QB * d) + c, nk)
            variant = jnp.where(nb > 0, 0, 1) if nblk > 1 else 0
            return rows, krows, variant

        def scores(idx, s_ref, nk=nk):
            rows, krows, variant = block_rows(idx)
            qs = stack_heads((q_ref[rows, :] * (1.0 / math.sqrt(HEAD_DIM))).astype(bf16))
            k = k_ref[krows, :].astype(bf16)
            for h in range(HEADS_PER_BLOCK):
                hrows = slice(h * QB, (h + 1) * QB)
                s_ref[hrows, :nk] = _dot_nt(qs[hrows], k) + bias_ref[variant, hrows, :nk]

        def softmax_pv(idx, s_ref, r=r, nk=nk):
            rows, krows, _ = block_rows(idx)
            s = s_ref[:, :nk]
            m = jnp.max(s, axis=-1, keepdims=True)
            p = jnp.exp(s - m)
            l = jnp.sum(p, axis=-1, keepdims=True)
            p = p.astype(bf16)
            vs = stack_heads(v_ref[krows, :].astype(bf16))
            acc_o[r, rows, :] = _dot(p[:QB], vs[:nk]) + _dot(p[QB:], vs[nk:])
            acc_m[r, rows, :] = jnp.where(head0, m[:QB], m[QB:])
            acc_l[r, rows, :] = jnp.where(head0, l[:QB], l[QB:])

        assert n_steps % 2 == 0
        scores(0, s0_ref)

        def step_pair(j, carry, n_steps=n_steps):
            scores(2 * j + 1, s1_ref)
            softmax_pv(2 * j, s0_ref)
            scores(jnp.minimum(2 * j + 2, n_steps - 1), s0_ref)
            softmax_pv(2 * j + 1, s1_ref)
            return carry

        lax.fori_loop(0, n_steps // 2, step_pair, 0)

    def merge(i, carry):
        rows = pl.ds(pl.multiple_of(i * Q_BLOCK, Q_BLOCK), Q_BLOCK)
        ms = [acc_m[r, rows, :] for r in range(len(DILATED_PATTERNS))]
        m_all = functools.reduce(jnp.maximum, ms)
        num = jnp.zeros((Q_BLOCK, LANES), f32)
        den = jnp.zeros((Q_BLOCK, LANES), f32)
        for r in range(len(DILATED_PATTERNS)):
            w = jnp.exp(ms[r] - m_all)
            num = num + w * acc_o[r, rows, :]
            den = den + w * acc_l[r, rows, :]
        o_ref[rows, :] = num / den
        return carry

    lax.fori_loop(0, S // Q_BLOCK, merge, 0)


def _dilated_attention(ha, slopes, n_heads):
    B, S, _ = ha.shape
    nhp = n_heads // HEADS_PER_BLOCK
    blk = lambda off: pl.BlockSpec((None, S, LANES), lambda b, h, sl, off=off: (b, 0, off + h))
    return pl.pallas_call(
        _dilated_kernel,
        name="dilated_attn",
        grid_spec=pltpu.PrefetchScalarGridSpec(
            num_scalar_prefetch=1,
            grid=(B, nhp),
            in_specs=[blk(0), blk(nhp), blk(2 * nhp)],
            out_specs=pl.BlockSpec((None, S, LANES), lambda b, h, sl: (b, 0, h)),
            scratch_shapes=[pltpu.VMEM((len(DILATED_PATTERNS), S, LANES), f32)] * 3
            + [pltpu.VMEM((2, 2 * Q_BLOCK, 2 * Q_BLOCK), f32)]
            + [pltpu.VMEM((2 * Q_BLOCK, 2 * Q_BLOCK), f32)] * 2),
        out_shape=jax.ShapeDtypeStruct((B, S, n_heads * HEAD_DIM), f32),
        compiler_params=_params(2),
    )(slopes, ha, ha, ha)


def _stick_kernel(q_ref, k_ref, v_ref, o_ref, later_ref):
    S = q_ref.shape[0]
    T = min(STICK_TILE, S)
    lane = lax.broadcasted_iota(jnp.int32, (T, LANES), 1)
    head0 = lane < HEAD_DIM
    krow = lax.broadcasted_iota(jnp.int32, (T, T), 0)
    kcol = lax.broadcasted_iota(jnp.int32, (T, T), 1)
    later_ref[...] = (krow > kcol).astype(bf16)
    causal = kcol < krow

    def stack_heads(x):
        return jnp.concatenate([jnp.where(head0, x, jnp.zeros_like(x)),
                                jnp.where(head0, jnp.zeros_like(x), x)], axis=0)

    def tile(qs, kb, nearer, acc, diagonal):
        krows = pl.ds(pl.multiple_of(kb * T, T), T)
        k = k_ref[krows, :]
        vs = stack_heads(v_ref[krows, :])
        zs = [_dot_nt(qs[h * T:(h + 1) * T], k) for h in range(HEADS_PER_BLOCK)]
        log_bs, sums, new_nearer = [], [], []
        for z in zs:
            log_b = jnp.minimum(z, 0.0) - jnp.log(1.0 + jnp.exp(-jnp.abs(z)))
            log_1mb = log_b - z
            if diagonal:
                log_1mb = jnp.where(causal, log_1mb, 0.0)
            hi = log_1mb.astype(bf16)
            lo = (log_1mb - hi.astype(f32)).astype(bf16)
            log_bs.append(log_b)
            sums.append(_dot(jnp.concatenate([hi, lo], axis=0), later_ref[...]))
            new_nearer.append(nearer[len(sums) * T - T:len(sums) * T]
                              + jnp.sum(log_1mb, axis=-1, keepdims=True))
        new_nearer = jnp.concatenate(new_nearer, axis=0)
        live = jnp.max(new_nearer) > STICK_UNDERFLOW
        for h in range(HEADS_PER_BLOCK):
            within = sums[h][:T] + sums[h][T:]
            a = jnp.exp(log_bs[h] + within + nearer[h * T:(h + 1) * T])
            if diagonal:
                a = jnp.where(causal, a, 0.0)
            acc = acc + _dot(a.astype(bf16), vs[h * T:(h + 1) * T])
        return live, new_nearer, acc

    def q_tile(n, carry):
        qrows = pl.ds(pl.multiple_of(n * T, T), T)
        qs = stack_heads(q_ref[qrows, :] * (1.0 / math.sqrt(HEAD_DIM)))
        first = tile(qs, n, jnp.zeros((2 * T, 1), f32), jnp.zeros((T, LANES), f32), True)

        def more(state):
            i, live, _, _ = state
            return jnp.logical_and(i < n, live)

        def farther(state):
            i, _, nearer, acc = state
            return (i + 1,) + tile(qs, n - 1 - i, nearer, acc, False)

        _, _, _, acc = lax.while_loop(more, farther, (jnp.int32(0),) + first)
        o_ref[qrows, :] = acc
        return carry

    lax.fori_loop(0, S // T, q_tile, 0)


def _stick_attention(hb, n_heads):
    B, S, _ = hb.shape
    nhp = n_heads // HEADS_PER_BLOCK
    T = min(STICK_TILE, S)
    blk = lambda off: pl.BlockSpec((None, S, LANES), lambda b, h, off=off: (b, 0, off + h))
    return pl.pallas_call(
        _stick_kernel,
        name="stick_attn",
        grid=(B, nhp),
        in_specs=[blk(0), blk(nhp), blk(2 * nhp)],
        out_specs=pl.BlockSpec((None, S, LANES), lambda b, h: (b, 0, h)),
        out_shape=jax.ShapeDtypeStruct((B, S, n_heads * HEAD_DIM), f32),
        scratch_shapes=[pltpu.VMEM((T, T), bf16)],
        compiler_params=_params(2),
    )(hb, hb, hb)


def _layer_norm(u, g, b):
    mu = jnp.mean(u, axis=-1, keepdims=True)
    var = jnp.mean(jnp.square(u - mu), axis=-1, keepdims=True)
    return (u - mu) * lax.rsqrt(var + LN_EPS) * g + b


def _rms_norm(o, g):
    return o * lax.rsqrt(jnp.mean(jnp.square(o), axis=-1, keepdims=True) + RMS_EPS) * g


def _mix_router_kernel(alpha, oa_ref, ob_ref, x_ref, wo_ref, ga_ref, gb_ref, lg_ref, lb_ref,
                       wr_ref, br_ref, x1_ref, idx_ref, gate_ref, pos_ref, cnt_ref, running,
                       wr_hi, wr_lo):
    tm = x_ref.shape[0]
    wa = oa_ref.shape[1]

    @pl.when(pl.program_id(0) == 0)
    def _():
        running[...] = jnp.zeros_like(running)
        w_hi = wr_ref[...].astype(bf16)
        wr_hi[...] = w_hi
        wr_lo[...] = (wr_ref[...] - w_hi.astype(f32)).astype(bf16)

    mix_a = _rms_norm(oa_ref[...], ga_ref[...]).astype(bf16)
    mix_b = _rms_norm(ob_ref[...], gb_ref[...]).astype(bf16)
    y = _dot(mix_a, wo_ref[:wa, :]) + _dot(mix_b, wo_ref[wa:, :])
    x1 = _layer_norm(alpha * x_ref[...] + y, lg_ref[...], lb_ref[...])
    x1_ref[...] = x1

    x_hi = x1.astype(bf16)
    x_lo = (x1 - x_hi.astype(f32)).astype(bf16)
    logits = (_dot(x_hi, wr_hi[...]) + (_dot(x_hi, wr_lo[...]) + _dot(x_lo, wr_hi[...]))) + br_ref[...]
    lane = lax.broadcasted_iota(jnp.int32, (tm, N_EXPERTS), 1)
    work = logits
    vals, sels, idxs = [], [], []
    for _ in range(TOP_K):
        m = jnp.max(work, axis=-1, keepdims=True)
        idx = jnp.min(jnp.where(work == m, lane, N_EXPERTS), axis=-1, keepdims=True)
        sel = lane == idx
        work = jnp.where(sel, -jnp.inf, work)
        vals.append(m)
        sels.append(sel)
        idxs.append(idx)
    exps = [jnp.exp(v - vals[0]) for v in vals]
    denom = functools.reduce(jnp.add, exps)

    chosen = functools.reduce(jnp.logical_or, sels)
    trow = lax.broadcasted_iota(jnp.int32, (tm, tm), 0)
    tcol = lax.broadcasted_iota(jnp.int32, (tm, tm), 1)
    earlier = (tcol < trow).astype(bf16)
    rank = _dot(earlier, chosen.astype(bf16)) + running[...]
    running[...] = running[...] + jnp.sum(chosen.astype(f32), axis=0, keepdims=True)
    cnt_ref[...] = running[...].astype(jnp.int32)

    out_lane = lax.broadcasted_iota(jnp.int32, (tm, LANES), 1)
    idx_out = jnp.zeros((tm, LANES), jnp.int32)
    gate_out = jnp.zeros((tm, LANES), f32)
    pos_out = jnp.zeros((tm, LANES), jnp.int32)
    for k in range(TOP_K):
        pos = jnp.sum(jnp.where(sels[k], rank, 0.0), axis=-1, keepdims=True).astype(jnp.int32)
        idx_out = jnp.where(out_lane == k, idxs[k], idx_out)
        gate_out = jnp.where(out_lane == k, exps[k] / denom, gate_out)
        pos_out = jnp.where(out_lane == k, pos, pos_out)
    idx_ref[...] = idx_out
    gate_ref[...] = gate_out
    pos_ref[...] = pos_out


def _mix_router(oa, ob, x, wo, layer, ga, gb, lg, lb, wr, br, alpha):
    T, D = x.shape
    W = oa.shape[1]
    tm = min(ROUTER_ROWS, T)
    rows = lambda n: pl.BlockSpec((tm, n), lambda i: (i, 0))
    full = lambda a: pl.BlockSpec(a.shape, lambda i: (0,) * a.ndim)
    return pl.pallas_call(
        functools.partial(_mix_router_kernel, alpha),
        name="mix_router",
        grid=(T // tm,),
        in_specs=[rows(W), rows(W), rows(D), pl.BlockSpec((None, D, D), lambda i: (layer, 0, 0)),
                  full(ga), full(gb), full(lg), full(lb), full(wr), full(br)],
        out_specs=[rows(D), rows(LANES), rows(LANES), rows(LANES),
                   pl.BlockSpec((1, N_EXPERTS), lambda i: (0, 0))],
        out_shape=[jax.ShapeDtypeStruct((T, D), f32),
                   jax.ShapeDtypeStruct((T, LANES), jnp.int32),
                   jax.ShapeDtypeStruct((T, LANES), f32),
                   jax.ShapeDtypeStruct((T, LANES), jnp.int32),
                   jax.ShapeDtypeStruct((1, N_EXPERTS), jnp.int32)],
        scratch_shapes=[pltpu.VMEM((1, N_EXPERTS), f32),
                        pltpu.VMEM(wr.shape, bf16), pltpu.VMEM(wr.shape, bf16)],
        compiler_params=_params(1),
    )(oa, ob, x, wo, ga, gb, lg, lb, wr, br)


def _dispatch_kernel(dest_ref, pad_start_ref, pad_len_ref, n_used_ref, x_ref, buf_ref,
                     zeros_ref, sem, zero_sem):
    tm = x_ref.shape[0]
    i = pl.program_id(0)
    base = i * (tm * TOP_K)
    n_blocks = buf_ref.shape[0] // EXPERT_ROWS

    @pl.when(i == 0)
    def _():
        zeros_ref[...] = jnp.zeros_like(zeros_ref)

        def pad_row(e, j):
            return pltpu.make_async_copy(
                zeros_ref.at[pl.ds(0, 1), :], buf_ref.at[pl.ds(pad_start_ref[e] + j, 1), :], zero_sem)

        def unused_block(b):
            rows = pl.ds(pl.multiple_of(b * EXPERT_ROWS, EXPERT_ROWS), EXPERT_ROWS)
            return pltpu.make_async_copy(zeros_ref, buf_ref.at[rows, :], zero_sem)

        def for_each(action):
            def per_expert(e, carry):
                def per_row(j, c):
                    action(pad_row(e, j))
                    return c
                return lax.fori_loop(0, pad_len_ref[e], per_row, carry)
            lax.fori_loop(0, N_EXPERTS, per_expert, 0)

            def per_block(b, carry):
                action(unused_block(b))
                return carry
            lax.fori_loop(n_used_ref[0], n_blocks, per_block, 0)

        for_each(lambda copy: copy.start())
        for_each(lambda copy: copy.wait())

    def start(t, carry):
        for k in range(TOP_K):
            pltpu.make_async_copy(
                x_ref.at[pl.ds(t, 1), :],
                buf_ref.at[pl.ds(dest_ref[base + t * TOP_K + k], 1), :], sem).start()
        return carry

    lax.fori_loop(0, tm, start, 0, unroll=4)
    all_rows = buf_ref.at[pl.ds(0, tm * TOP_K), :]
    pltpu.make_async_copy(all_rows, all_rows, sem).wait()


def _dispatch(x1, dest_flat, pad_start, pad_len, n_used, n_rows):
    T, D = x1.shape
    tm = min(DISPATCH_ROWS, T)
    return pl.pallas_call(
        _dispatch_kernel,
        name="moe_dispatch",
        grid_spec=pltpu.PrefetchScalarGridSpec(
            num_scalar_prefetch=4,
            grid=(T // tm,),
            in_specs=[pl.BlockSpec((tm, D), lambda i, *_: (i, 0))],
            out_specs=pl.BlockSpec(memory_space=pl.ANY),
            scratch_shapes=[pltpu.VMEM((EXPERT_ROWS, D), f32),
                            pltpu.SemaphoreType.DMA(()), pltpu.SemaphoreType.DMA(())]),
        out_shape=jax.ShapeDtypeStruct((n_rows, D), f32),
        compiler_params=_params(1),
    )(dest_flat, pad_start, pad_len, n_used, x1)


def _expert_matmul_kernel(activation, blk_expert_ref, n_used_ref, x_ref, w_ref, b_ref, o_ref, wq_ref):
    i = pl.program_id(0)
    used = i < n_used_ref[0]
    first_of_expert = jnp.logical_or(
        i == 0, blk_expert_ref[i] != blk_expert_ref[jnp.maximum(i - 1, 0)])

    @pl.when(jnp.logical_and(used, first_of_expert))
    def _():
        chunk = math.gcd(WEIGHT_CAST_ROWS, w_ref.shape[0])

        def cast_rows(c, carry):
            rows = pl.ds(pl.multiple_of(c * chunk, chunk), chunk)
            wq_ref[rows, :] = w_ref[rows, :].astype(bf16)
            return carry
        lax.fori_loop(0, w_ref.shape[0] // chunk, cast_rows, 0)

    @pl.when(used)
    def _():
        h = _dot(x_ref[...].astype(bf16), wq_ref[...]) + b_ref[...]
        o_ref[...] = activation(h).astype(o_ref.dtype)

    @pl.when(jnp.logical_not(used))
    def _():
        o_ref[...] = jnp.zeros_like(o_ref)


def _clamped_swiglu(h):
    d_ff = h.shape[1] // 2
    gate = jnp.minimum(h[:, :d_ff], SWIGLU_LIMIT)
    up = jnp.clip(h[:, d_ff:], -SWIGLU_LIMIT, SWIGLU_LIMIT)
    return (up + 1.0) * (gate * jax.nn.sigmoid(gate * SWIGLU_ALPHA))


def _expert_matmul(x_buf, blk_expert, n_used, w, b, layer, activation, out_dtype, name):
    R, K = x_buf.shape
    L, E, _, N = w.shape
    n_out = jax.eval_shape(activation, jax.ShapeDtypeStruct((EXPERT_ROWS, N), f32)).shape[1]
    tm = EXPERT_ROWS
    used = lambda i, be, nu: jnp.minimum(i, nu[0] - 1)
    expert = lambda i, be, nu: (layer, be[i], 0, 0)
    return pl.pallas_call(
        functools.partial(_expert_matmul_kernel, activation),
        name=name,
        grid_spec=pltpu.PrefetchScalarGridSpec(
            num_scalar_prefetch=2,
            grid=(R // tm,),
            in_specs=[pl.BlockSpec((tm, K), lambda i, be, nu: (used(i, be, nu), 0)),
                      pl.BlockSpec((None, None, K, N), expert),
                      pl.BlockSpec((None, None, 1, N), expert)],
            out_specs=pl.BlockSpec((tm, n_out), lambda i, be, nu: (i, 0)),
            scratch_shapes=[pltpu.VMEM((K, N), bf16)]),
        out_shape=jax.ShapeDtypeStruct((R, n_out), out_dtype),
        compiler_params=_params(1),
    )(blk_expert, n_used, x_buf, w, b.reshape(L, E, 1, N))


def _expert_ffn(x_buf, blk_expert, n_used, wgu, bgu, wd, bd, layer):
    act = _expert_matmul(x_buf, blk_expert, n_used, wgu, bgu, layer, _clamped_swiglu, bf16,
                         "expert_gate_up")
    return _expert_matmul(act, blk_expert, n_used, wd, bd, layer, lambda h: h, f32, "expert_down")


def _combine_kernel(alpha, dest_ref, y_hbm, x_ref, gate_ref, lg_ref, lb_ref, o_ref, rows_buf, sems):
    tm = x_ref.shape[0]
    i = pl.program_id(0)
    n = pl.num_programs(0)

    def start_tile(tile, slot):
        def body(t, carry):
            for k in range(TOP_K):
                pltpu.make_async_copy(
                    y_hbm.at[pl.ds(dest_ref[(tile * tm + t) * TOP_K + k], 1), :],
                    rows_buf.at[slot, k, pl.ds(t, 1), :], sems.at[slot]).start()
            return carry
        lax.fori_loop(0, tm, body, 0, unroll=4)

    slot = i % 2

    @pl.when(i == 0)
    def _():
        start_tile(0, 0)

    @pl.when(i + 1 < n)
    def _():
        start_tile(i + 1, 1 - slot)

    for k in range(TOP_K):
        pltpu.make_async_copy(y_hbm.at[pl.ds(0, tm), :], rows_buf.at[slot, k], sems.at[slot]).wait()

    y = jnp.zeros(x_ref.shape, f32)
    for k in range(TOP_K):
        y = y + gate_ref[:, k:k + 1] * rows_buf[slot, k]
    o_ref[...] = _layer_norm(alpha * x_ref[...] + y, lg_ref[...], lb_ref[...])


def _combine(y_buf, dest_flat, x1, gates, lg, lb, alpha):
    T, D = x1.shape
    tm = min(COMBINE_ROWS, T)
    return pl.pallas_call(
        functools.partial(_combine_kernel, alpha),
        name="moe_combine",
        grid_spec=pltpu.PrefetchScalarGridSpec(
            num_scalar_prefetch=1,
            grid=(T // tm,),
            in_specs=[pl.BlockSpec(memory_space=pl.ANY),
                      pl.BlockSpec((tm, D), lambda i, d: (i, 0)),
                      pl.BlockSpec((tm, LANES), lambda i, d: (i, 0)),
                      pl.BlockSpec((1, D), lambda i, d: (0, 0)),
                      pl.BlockSpec((1, D), lambda i, d: (0, 0))],
            out_specs=pl.BlockSpec((tm, D), lambda i, d: (i, 0)),
            scratch_shapes=[pltpu.VMEM((2, TOP_K, tm, D), f32),
                            pltpu.SemaphoreType.DMA((2,))]),
        out_shape=jax.ShapeDtypeStruct((T, D), f32),
        compiler_params=_params(1),
    )(dest_flat, y_buf, x1, gates, lg, lb)


def _moe(x1, idx, gates, pos, counts, wgu, bgu, wd, bd, layer, lg, lb, alpha):
    T, D = x1.shape
    n_assign = T * TOP_K
    counts = counts.reshape(N_EXPERTS)
    padded = (counts + EXPERT_ROWS - 1) // EXPERT_ROWS * EXPERT_ROWS
    group_end = jnp.cumsum(padded)
    group_start = group_end - padded
    experts = jnp.arange(N_EXPERTS, dtype=jnp.int32)
    start_of = jnp.sum(jnp.where(idx[:, :TOP_K, None] == experts, group_start, 0), axis=-1)
    dest = (start_of + pos[:, :TOP_K]).reshape(n_assign).astype(jnp.int32)
    n_blocks = -(-n_assign // EXPERT_ROWS) + N_EXPERTS
    block_row = jnp.arange(n_blocks, dtype=jnp.int32)[:, None] * EXPERT_ROWS
    blk_expert = jnp.minimum(jnp.sum(group_end[None, :] <= block_row, axis=-1),
                             N_EXPERTS - 1).astype(jnp.int32)
    n_used = (group_end[-1:] // EXPERT_ROWS).astype(jnp.int32)
    x_buf = _dispatch(x1, dest, (group_start + counts).astype(jnp.int32),
                      (padded - counts).astype(jnp.int32), n_used, n_blocks * EXPERT_ROWS)
    y_buf = _expert_ffn(x_buf, blk_expert, n_used, wgu, bgu, wd, bd, layer)
    return _combine(y_buf, dest, x1, gates, lg, lb, alpha)


def kernel(x, w_in, g_mix_a, g_mix_b, w_out, ln1_g, ln1_b, w_router, b_router,
           w_gate_up, b_gate_up, w_down, b_down, ln2_g, ln2_b):
    B, S, D = x.shape
    depth = w_in.shape[0]
    wa = g_mix_a.shape[1]
    wb = g_mix_b.shape[1]
    heads_a = wa // HEAD_DIM
    heads_b = wb // HEAD_DIM
    alpha = (2.0 * depth) ** 0.25
    slopes = jnp.asarray([2.0 ** (-8.0 * (h + 1) / heads_a) for h in range(heads_a)], f32)
    row = lambda a: a.reshape(1, -1)

    assert wa == wb, "the q/k/v projection is split into two equal column halves"
    w_in_b = w_in.astype(bf16)
    w_out_b = w_out.astype(bf16)

    xt = x.reshape(B * S, D)
    for l in range(depth):
        ha = _proj(xt, w_in_b, l, 0, f32).reshape(B, S, 3 * wa)
        hb = _proj(xt, w_in_b, l, 1, bf16).reshape(B, S, 3 * wb)
        oa = _dilated_attention(ha, slopes, heads_a).reshape(B * S, wa)
        ob = _stick_attention(hb, heads_b).reshape(B * S, wb)
        x1, idx, gates, pos, counts = _mix_router(
            oa, ob, xt, w_out_b, l, row(g_mix_a[l]), row(g_mix_b[l]),
            row(ln1_g[l]), row(ln1_b[l]), w_router[l], row(b_router[l]), alpha)
        xt = _moe(x1, idx, gates, pos, counts, w_gate_up, b_gate_up, w_down, b_down, l,
                  row(ln2_g[l]), row(ln2_b[l]), alpha)
    return xt.reshape(B, S, D)
```

```python
import functools
import math

import jax
import jax.numpy as jnp
from jax import lax
from jax.experimental import pallas as pl
from jax.experimental.pallas import tpu as pltpu

HEAD_DIM = 64
HEADS_PER_BLOCK = 2
LANES = HEADS_PER_BLOCK * HEAD_DIM
Q_BLOCK = 128
DILATED_PATTERNS = ((128, 1), (512, 4), (2048, 16))
N_EXPERTS = 32
TOP_K = 4
SWIGLU_LIMIT = 7.0
SWIGLU_ALPHA = 1.702
LN_EPS = 1e-5
RMS_EPS = 1e-6
MASKED = -1e30

VMEM_LIMIT = 56 * 1024 * 1024
PROJ_ROWS = 256
ROUTER_ROWS = 256
DISPATCH_ROWS = 128
EXPERT_ROWS = 256
WEIGHT_CAST_ROWS = 256
COMBINE_ROWS = 128
STICK_TILE = 256
LOG2_E = 1.4426950408889634
QUERY_SCALE = LOG2_E / math.sqrt(HEAD_DIM)
STICK_UNDERFLOW_LOG2 = -176.0

f32 = jnp.float32
bf16 = jnp.bfloat16


def _params(n_grid_dims):
    return pltpu.CompilerParams(
        dimension_semantics=("arbitrary",) * n_grid_dims, vmem_limit_bytes=VMEM_LIMIT)


def _dot_nt(a, b):
    return lax.dot_general(a, b, (((1,), (1,)), ((), ())), preferred_element_type=f32)


def _dot(a, b):
    return jnp.dot(a, b, preferred_element_type=f32)


def _proj_kernel(n_query_cols, x_ref, w_ref, o_ref):
    h = _dot(x_ref[...].astype(bf16), w_ref[...])
    query = lax.broadcasted_iota(jnp.int32, (1, h.shape[1]), 1) < n_query_cols
    o_ref[...] = (h * jnp.where(query, QUERY_SCALE, 1.0)).astype(o_ref.dtype)


def _proj(x, w, layer, half, out_dtype):
    T, K = x.shape
    N = w.shape[2] // 2
    tm = min(PROJ_ROWS, T)
    return pl.pallas_call(
        functools.partial(_proj_kernel, N // 3),
        name="proj",
        grid=(T // tm,),
        in_specs=[pl.BlockSpec((tm, K), lambda i: (i, 0)),
                  pl.BlockSpec((None, K, N), lambda i: (layer, 0, half))],
        out_specs=pl.BlockSpec((tm, N), lambda i: (i, 0)),
        out_shape=jax.ShapeDtypeStruct((T, N), out_dtype),
        compiler_params=_params(1),
    )(x, w)


def _dilated_kernel(slopes_ref, q_ref, k_ref, v_ref, o_ref, acc_o, acc_m, acc_l, bias_ref,
                    s0_ref, s1_ref):
    S = q_ref.shape[0]
    QB = Q_BLOCK
    hp = pl.program_id(1)
    head0 = lax.broadcasted_iota(jnp.int32, (QB, LANES), 1) < HEAD_DIM

    def stack_heads(x):
        first = lax.broadcasted_iota(jnp.int32, x.shape, 1) < HEAD_DIM
        return jnp.concatenate([jnp.where(first, x, jnp.zeros_like(x)),
                                jnp.where(first, jnp.zeros_like(x), x)], axis=0)

    for r, (window, d) in enumerate(DILATED_PATTERNS):
        assert window // d == QB
        nblk = (S // d) // QB
        nk = 2 * QB if nblk > 1 else QB
        srow = lax.broadcasted_iota(jnp.int32, (2 * QB, nk), 0)
        scol = lax.broadcasted_iota(jnp.int32, (2 * QB, nk), 1)
        second = srow >= QB
        coef = jnp.where(second, -(slopes_ref[hp * HEADS_PER_BLOCK + 1] * float(d)),
                         -(slopes_ref[hp * HEADS_PER_BLOCK] * float(d)))
        for variant, offset in enumerate((nk - QB, 0)):
            delta = jnp.where(second, srow - QB, srow) - scol + offset
            ok = jnp.logical_and(delta >= 0, delta <= QB)
            bias_ref[variant, :, :nk] = jnp.where(ok, (coef * delta.astype(f32)) * LOG2_E, MASKED)

        n_steps = d * nblk

        def block_rows(idx, d=d, nblk=nblk, nk=nk):
            c = idx // nblk
            nb = idx % nblk
            strided = lambda start, n: pl.ds(start, n, stride=d) if d > 1 else pl.ds(start, n)
            rows = strided(nb * (QB * d) + c, QB)
            krows = strided(jnp.maximum(nb - (nk // QB - 1), 0) * (QB * d) + c, nk)
            variant = jnp.where(nb > 0, 0, 1) if nblk > 1 else 0
            return rows, krows, variant

        def scores(idx, s_ref, nk=nk):
            rows, krows, variant = block_rows(idx)
            qs = stack_heads(q_ref[rows, :].astype(bf16))
            k = k_ref[krows, :].astype(bf16)
            for h in range(HEADS_PER_BLOCK):
                hrows = slice(h * QB, (h + 1) * QB)
                s_ref[hrows, :nk] = _dot_nt(qs[hrows], k) + bias_ref[variant, hrows, :nk]

        def softmax_pv(idx, s_ref, r=r, nk=nk):
            rows, krows, _ = block_rows(idx)
            s = s_ref[:, :nk]
            m = jnp.max(s, axis=-1, keepdims=True)
            p = jnp.exp2(s - m)
            l = jnp.sum(p, axis=-1, keepdims=True)
            p = p.astype(bf16)
            vs = stack_heads(v_ref[krows, :].astype(bf16))
            acc_o[r, rows, :] = _dot(p[:QB], vs[:nk]) + _dot(p[QB:], vs[nk:])
            acc_m[r, rows, :] = jnp.where(head0, m[:QB], m[QB:])
            acc_l[r, rows, :] = jnp.where(head0, l[:QB], l[QB:])

        assert n_steps % 2 == 0
        scores(0, s0_ref)

        def step_pair(j, carry, n_steps=n_steps):
            scores(2 * j + 1, s1_ref)
            softmax_pv(2 * j, s0_ref)
            scores(jnp.minimum(2 * j + 2, n_steps - 1), s0_ref)
            softmax_pv(2 * j + 1, s1_ref)
            return carry

        lax.fori_loop(0, n_steps // 2, step_pair, 0, unroll=4)

    def merge(i, carry):
        rows = pl.ds(pl.multiple_of(i * Q_BLOCK, Q_BLOCK), Q_BLOCK)
        ms = [acc_m[r, rows, :] for r in range(len(DILATED_PATTERNS))]
        m_all = functools.reduce(jnp.maximum, ms)
        num = jnp.zeros((Q_BLOCK, LANES), f32)
        den = jnp.zeros((Q_BLOCK, LANES), f32)
        for r in range(len(DILATED_PATTERNS)):
            w = jnp.exp2(ms[r] - m_all)
            num = num + w * acc_o[r, rows, :]
            den = den + w * acc_l[r, rows, :]
        o_ref[rows, :] = num / den
        return carry

    lax.fori_loop(0, S // Q_BLOCK, merge, 0)


def _dilated_attention(ha, slopes, n_heads):
    B, S, _ = ha.shape
    nhp = n_heads // HEADS_PER_BLOCK
    blk = lambda off: pl.BlockSpec((None, S, LANES), lambda b, h, sl, off=off: (b, 0, off + h))
    return pl.pallas_call(
        _dilated_kernel,
        name="dilated_attn",
        grid_spec=pltpu.PrefetchScalarGridSpec(
            num_scalar_prefetch=1,
            grid=(B, nhp),
            in_specs=[blk(0), blk(nhp), blk(2 * nhp)],
            out_specs=pl.BlockSpec((None, S, LANES), lambda b, h, sl: (b, 0, h)),
            scratch_shapes=[pltpu.VMEM((len(DILATED_PATTERNS), S, LANES), f32)] * 3
            + [pltpu.VMEM((2, 2 * Q_BLOCK, 2 * Q_BLOCK), f32)]
            + [pltpu.VMEM((2 * Q_BLOCK, 2 * Q_BLOCK), f32)] * 2),
        out_shape=jax.ShapeDtypeStruct((B, S, n_heads * HEAD_DIM), f32),
        compiler_params=_params(2),
    )(slopes, ha, ha, ha)


def _stick_kernel(q_ref, k_ref, v_ref, o_ref, later_ref):
    S = q_ref.shape[0]
    T = min(STICK_TILE, S)
    lane = lax.broadcasted_iota(jnp.int32, (T, LANES), 1)
    head0 = lane < HEAD_DIM
    krow = lax.broadcasted_iota(jnp.int32, (T, T), 0)
    kcol = lax.broadcasted_iota(jnp.int32, (T, T), 1)
    later_ref[...] = (krow > kcol).astype(bf16)
    causal = kcol < krow

    def stack_heads(x):
        return jnp.concatenate([jnp.where(head0, x, jnp.zeros_like(x)),
                                jnp.where(head0, jnp.zeros_like(x), x)], axis=0)

    def tile(qs, kb, nearer, acc, diagonal):
        krows = pl.ds(pl.multiple_of(kb * T, T), T)
        k = k_ref[krows, :]
        vs = stack_heads(v_ref[krows, :])
        zs = [_dot_nt(qs[h * T:(h + 1) * T], k) for h in range(HEADS_PER_BLOCK)]
        log_bs, withins, new_nearer = [], [], []
        for z in zs:
            log_b = jnp.minimum(z, 0.0) - jnp.log2(1.0 + jnp.exp2(-jnp.abs(z)))
            log_1mb = log_b - z
            if diagonal:
                log_1mb = jnp.where(causal, log_1mb, 0.0)
            log_bs.append(log_b)
            withins.append(_dot(log_1mb.astype(bf16), later_ref[...]))
            new_nearer.append(nearer[len(log_bs) * T - T:len(log_bs) * T]
                              + jnp.sum(log_1mb, axis=-1, keepdims=True))
        new_nearer = jnp.concatenate(new_nearer, axis=0)
        live = jnp.max(new_nearer) > STICK_UNDERFLOW_LOG2
        for h in range(HEADS_PER_BLOCK):
            a = jnp.exp2(log_bs[h] + withins[h] + nearer[h * T:(h + 1) * T])
            if diagonal:
                a = jnp.where(causal, a, 0.0)
            acc = acc + _dot(a.astype(bf16), vs[h * T:(h + 1) * T])
        return live, new_nearer, acc

    def q_tile(n, has_previous):
        qrows = pl.ds(pl.multiple_of(n * T, T), T)
        qs = stack_heads(q_ref[qrows, :])
        state = tile(qs, n, jnp.zeros((2 * T, 1), f32), jnp.zeros((T, LANES), f32), True)
        walked = 0
        if has_previous:
            state = tile(qs, n - 1, state[1], state[2], False)
            walked = 1

        def more(state):
            i, live, _, _ = state
            return jnp.logical_and(i < n, live)

        def farther(state):
            i, _, nearer, acc = state
            return (i + 1,) + tile(qs, n - 1 - i, nearer, acc, False)

        _, _, _, acc = lax.while_loop(more, farther, (jnp.int32(walked),) + state)
        o_ref[qrows, :] = acc

    q_tile(0, False)

    def later_q_tile(n, carry):
        q_tile(n, True)
        return carry

    lax.fori_loop(1, S // T, later_q_tile, 0)


def _stick_attention(hb, n_heads):
    B, S, _ = hb.shape
    nhp = n_heads // HEADS_PER_BLOCK
    T = min(STICK_TILE, S)
    blk = lambda off: pl.BlockSpec((None, S, LANES), lambda b, h, off=off: (b, 0, off + h))
    return pl.pallas_call(
        _stick_kernel,
        name="stick_attn",
        grid=(B, nhp),
        in_specs=[blk(0), blk(nhp), blk(2 * nhp)],
        out_specs=pl.BlockSpec((None, S, LANES), lambda b, h: (b, 0, h)),
        out_shape=jax.ShapeDtypeStruct((B, S, n_heads * HEAD_DIM), f32),
        scratch_shapes=[pltpu.VMEM((T, T), bf16)],
        compiler_params=_params(2),
    )(hb, hb, hb)


def _layer_norm(u, g, b):
    mu = jnp.mean(u, axis=-1, keepdims=True)
    var = jnp.mean(jnp.square(u - mu), axis=-1, keepdims=True)
    return (u - mu) * lax.rsqrt(var + LN_EPS) * g + b


def _rms_norm(o, g):
    return o * lax.rsqrt(jnp.mean(jnp.square(o), axis=-1, keepdims=True) + RMS_EPS) * g


def _mix_router_kernel(alpha, oa_ref, ob_ref, x_ref, wo_ref, ga_ref, gb_ref, lg_ref, lb_ref,
                       wr_ref, br_ref, x1_ref, idx_ref, gate_ref, pos_ref, cnt_ref, running,
                       wr_hi, wr_lo):
    tm = x_ref.shape[0]
    wa = oa_ref.shape[1]

    @pl.when(pl.program_id(0) == 0)
    def _():
        running[...] = jnp.zeros_like(running)
        w_hi = wr_ref[...].astype(bf16)
        wr_hi[...] = w_hi
        wr_lo[...] = (wr_ref[...] - w_hi.astype(f32)).astype(bf16)

    mix_a = _rms_norm(oa_ref[...], ga_ref[...]).astype(bf16)
    mix_b = _rms_norm(ob_ref[...], gb_ref[...]).astype(bf16)
    y = _dot(mix_a, wo_ref[:wa, :]) + _dot(mix_b, wo_ref[wa:, :])
    x1 = _layer_norm(alpha * x_ref[...] + y, lg_ref[...], lb_ref[...])
    x1_ref[...] = x1

    x_hi = x1.astype(bf16)
    x_lo = (x1 - x_hi.astype(f32)).astype(bf16)
    logits = (_dot(x_hi, wr_hi[...]) + (_dot(x_hi, wr_lo[...]) + _dot(x_lo, wr_hi[...]))) + br_ref[...]
    lane = lax.broadcasted_iota(jnp.int32, (tm, N_EXPERTS), 1)
    work = logits
    vals, sels, idxs = [], [], []
    for _ in range(TOP_K):
        m = jnp.max(work, axis=-1, keepdims=True)
        idx = jnp.min(jnp.where(work == m, lane, N_EXPERTS), axis=-1, keepdims=True)
        sel = lane == idx
        work = jnp.where(sel, -jnp.inf, work)
        vals.append(m)
        sels.append(sel)
        idxs.append(idx)
    exps = [jnp.exp(v - vals[0]) for v in vals]
    denom = functools.reduce(jnp.add, exps)

    chosen = functools.reduce(jnp.logical_or, sels)
    trow = lax.broadcasted_iota(jnp.int32, (tm, tm), 0)
    tcol = lax.broadcasted_iota(jnp.int32, (tm, tm), 1)
    earlier = (tcol < trow).astype(bf16)
    rank = _dot(earlier, chosen.astype(bf16)) + running[...]
    running[...] = running[...] + jnp.sum(chosen.astype(f32), axis=0, keepdims=True)
    cnt_ref[...] = running[...].astype(jnp.int32)

    out_lane = lax.broadcasted_iota(jnp.int32, (tm, LANES), 1)
    idx_out = jnp.zeros((tm, LANES), jnp.int32)
    gate_out = jnp.zeros((tm, LANES), f32)
    pos_out = jnp.zeros((tm, LANES), jnp.int32)
    for k in range(TOP_K):
        pos = jnp.sum(jnp.where(sels[k], rank, 0.0), axis=-1, keepdims=True).astype(jnp.int32)
        idx_out = jnp.where(out_lane == k, idxs[k], idx_out)
        gate_out = jnp.where(out_lane == k, exps[k] / denom, gate_out)
        pos_out = jnp.where(out_lane == k, pos, pos_out)
    idx_ref[...] = idx_out
    gate_ref[...] = gate_out
    pos_ref[...] = pos_out


def _mix_router(oa, ob, x, wo, layer, ga, gb, lg, lb, wr, br, alpha):
    T, D = x.shape
    W = oa.shape[1]
    tm = min(ROUTER_ROWS, T)
    rows = lambda n: pl.BlockSpec((tm, n), lambda i: (i, 0))
    full = lambda a: pl.BlockSpec(a.shape, lambda i: (0,) * a.ndim)
    return pl.pallas_call(
        functools.partial(_mix_router_kernel, alpha),
        name="mix_router",
        grid=(T // tm,),
        in_specs=[rows(W), rows(W), rows(D), pl.BlockSpec((None, D, D), lambda i: (layer, 0, 0)),
                  full(ga), full(gb), full(lg), full(lb), full(wr), full(br)],
        out_specs=[rows(D), rows(LANES), rows(LANES), rows(LANES),
                   pl.BlockSpec((1, N_EXPERTS), lambda i: (0, 0))],
        out_shape=[jax.ShapeDtypeStruct((T, D), f32),
                   jax.ShapeDtypeStruct((T, LANES), jnp.int32),
                   jax.ShapeDtypeStruct((T, LANES), f32),
                   jax.ShapeDtypeStruct((T, LANES), jnp.int32),
                   jax.ShapeDtypeStruct((1, N_EXPERTS), jnp.int32)],
        scratch_shapes=[pltpu.VMEM((1, N_EXPERTS), f32),
                        pltpu.VMEM(wr.shape, bf16), pltpu.VMEM(wr.shape, bf16)],
        compiler_params=_params(1),
    )(oa, ob, x, wo, ga, gb, lg, lb, wr, br)


def _dispatch_kernel(dest_ref, pad_start_ref, pad_len_ref, n_used_ref, x_ref, buf_ref,
                     zeros_ref, sem, zero_sem):
    tm = x_ref.shape[0]
    i = pl.program_id(0)
    base = i * (tm * TOP_K)
    n_blocks = buf_ref.shape[0] // EXPERT_ROWS

    @pl.when(i == 0)
    def _():
        zeros_ref[...] = jnp.zeros_like(zeros_ref)

        def pad_row(e, j):
            return pltpu.make_async_copy(
                zeros_ref.at[pl.ds(0, 1), :], buf_ref.at[pl.ds(pad_start_ref[e] + j, 1), :], zero_sem)

        def unused_block(b):
            rows = pl.ds(pl.multiple_of(b * EXPERT_ROWS, EXPERT_ROWS), EXPERT_ROWS)
            return pltpu.make_async_copy(zeros_ref, buf_ref.at[rows, :], zero_sem)

        def for_each(action):
            def per_expert(e, carry):
                def per_row(j, c):
                    action(pad_row(e, j))
                    return c
                return lax.fori_loop(0, pad_len_ref[e], per_row, carry)
            lax.fori_loop(0, N_EXPERTS, per_expert, 0)

            def per_block(b, carry):
                action(unused_block(b))
                return carry
            lax.fori_loop(n_used_ref[0], n_blocks, per_block, 0)

        for_each(lambda copy: copy.start())
        for_each(lambda copy: copy.wait())

    def start(t, carry):
        for k in range(TOP_K):
            pltpu.make_async_copy(
                x_ref.at[pl.ds(t, 1), :],
                buf_ref.at[pl.ds(dest_ref[base + t * TOP_K + k], 1), :], sem).start()
        return carry

    lax.fori_loop(0, tm, start, 0, unroll=4)
    all_rows = buf_ref.at[pl.ds(0, tm * TOP_K), :]
    pltpu.make_async_copy(all_rows, all_rows, sem).wait()


def _dispatch(x1, dest_flat, pad_start, pad_len, n_used, n_rows):
    T, D = x1.shape
    tm = min(DISPATCH_ROWS, T)
    return pl.pallas_call(
        _dispatch_kernel,
        name="moe_dispatch",
        grid_spec=pltpu.PrefetchScalarGridSpec(
            num_scalar_prefetch=4,
            grid=(T // tm,),
            in_specs=[pl.BlockSpec((tm, D), lambda i, *_: (i, 0))],
            out_specs=pl.BlockSpec(memory_space=pl.ANY),
            scratch_shapes=[pltpu.VMEM((EXPERT_ROWS, D), f32),
                            pltpu.SemaphoreType.DMA(()), pltpu.SemaphoreType.DMA(())]),
        out_shape=jax.ShapeDtypeStruct((n_rows, D), f32),
        compiler_params=_params(1),
    )(dest_flat, pad_start, pad_len, n_used, x1)


def _expert_matmul_kernel(activation, blk_expert_ref, n_used_ref, x_ref, w_ref, b_ref, o_ref, wq_ref):
    i = pl.program_id(0)
    used = i < n_used_ref[0]
    first_of_expert = jnp.logical_or(
        i == 0, blk_expert_ref[i] != blk_expert_ref[jnp.maximum(i - 1, 0)])

    @pl.when(jnp.logical_and(used, first_of_expert))
    def _():
        chunk = math.gcd(WEIGHT_CAST_ROWS, w_ref.shape[0])

        def cast_rows(c, carry):
            rows = pl.ds(pl.multiple_of(c * chunk, chunk), chunk)
            wq_ref[rows, :] = w_ref[rows, :].astype(bf16)
            return carry
        lax.fori_loop(0, w_ref.shape[0] // chunk, cast_rows, 0)

    @pl.when(used)
    def _():
        h = _dot(x_ref[...].astype(bf16), wq_ref[...]) + b_ref[...]
        o_ref[...] = activation(h).astype(o_ref.dtype)

    @pl.when(jnp.logical_not(used))
    def _():
        o_ref[...] = jnp.zeros_like(o_ref)


def _clamped_swiglu(h):
    d_ff = h.shape[1] // 2
    gate = jnp.minimum(h[:, :d_ff], SWIGLU_LIMIT)
    up = jnp.clip(h[:, d_ff:], -SWIGLU_LIMIT, SWIGLU_LIMIT)
    return (up + 1.0) * (gate * jax.nn.sigmoid(gate * SWIGLU_ALPHA))


def _expert_matmul(x_buf, blk_expert, n_used, w, b, layer, activation, out_dtype, name):
    R, K = x_buf.shape
    L, E, _, N = w.shape
    n_out = jax.eval_shape(activation, jax.ShapeDtypeStruct((EXPERT_ROWS, N), f32)).shape[1]
    tm = EXPERT_ROWS
    used = lambda i, be, nu: jnp.minimum(i, nu[0] - 1)
    expert = lambda i, be, nu: (layer, be[i], 0, 0)
    return pl.pallas_call(
        functools.partial(_expert_matmul_kernel, activation),
        name=name,
        grid_spec=pltpu.PrefetchScalarGridSpec(
            num_scalar_prefetch=2,
            grid=(R // tm,),
            in_specs=[pl.BlockSpec((tm, K), lambda i, be, nu: (used(i, be, nu), 0)),
                      pl.BlockSpec((None, None, K, N), expert),
                      pl.BlockSpec((None, None, 1, N), expert)],
            out_specs=pl.BlockSpec((tm, n_out), lambda i, be, nu: (i, 0)),
            scratch_shapes=[pltpu.VMEM((K, N), bf16)]),
        out_shape=jax.ShapeDtypeStruct((R, n_out), out_dtype),
        compiler_params=_params(1),
    )(blk_expert, n_used, x_buf, w, b.reshape(L, E, 1, N))


def _expert_ffn(x_buf, blk_expert, n_used, wgu, bgu, wd, bd, layer):
    act = _expert_matmul(x_buf, blk_expert, n_used, wgu, bgu, layer, _clamped_swiglu, bf16,
                         "expert_gate_up")
    return _expert_matmul(act, blk_expert, n_used, wd, bd, layer, lambda h: h, f32, "expert_down")


def _combine_kernel(alpha, dest_ref, y_hbm, x_ref, gate_ref, lg_ref, lb_ref, o_ref, rows_buf, sems):
    tm = x_ref.shape[0]
    i = pl.program_id(0)
    n = pl.num_programs(0)

    def start_tile(tile, slot):
        def body(t, carry):
            for k in range(TOP_K):
                pltpu.make_async_copy(
                    y_hbm.at[pl.ds(dest_ref[(tile * tm + t) * TOP_K + k], 1), :],
                    rows_buf.at[slot, k, pl.ds(t, 1), :], sems.at[slot]).start()
            return carry
        lax.fori_loop(0, tm, body, 0, unroll=4)

    slot = i % 2

    @pl.when(i == 0)
    def _():
        start_tile(0, 0)

    @pl.when(i + 1 < n)
    def _():
        start_tile(i + 1, 1 - slot)

    for k in range(TOP_K):
        pltpu.make_async_copy(y_hbm.at[pl.ds(0, tm), :], rows_buf.at[slot, k], sems.at[slot]).wait()

    y = jnp.zeros(x_ref.shape, f32)
    for k in range(TOP_K):
        y = y + gate_ref[:, k:k + 1] * rows_buf[slot, k]
    o_ref[...] = _layer_norm(alpha * x_ref[...] + y, lg_ref[...], lb_ref[...])


def _combine(y_buf, dest_flat, x1, gates, lg, lb, alpha):
    T, D = x1.shape
    tm = min(COMBINE_ROWS, T)
    return pl.pallas_call(
        functools.partial(_combine_kernel, alpha),
        name="moe_combine",
        grid_spec=pltpu.PrefetchScalarGridSpec(
            num_scalar_prefetch=1,
            grid=(T // tm,),
            in_specs=[pl.BlockSpec(memory_space=pl.ANY),
                      pl.BlockSpec((tm, D), lambda i, d: (i, 0)),
                      pl.BlockSpec((tm, LANES), lambda i, d: (i, 0)),
                      pl.BlockSpec((1, D), lambda i, d: (0, 0)),
                      pl.BlockSpec((1, D), lambda i, d: (0, 0))],
            out_specs=pl.BlockSpec((tm, D), lambda i, d: (i, 0)),
            scratch_shapes=[pltpu.VMEM((2, TOP_K, tm, D), f32),
                            pltpu.SemaphoreType.DMA((2,))]),
        out_shape=jax.ShapeDtypeStruct((T, D), f32),
        compiler_params=_params(1),
    )(dest_flat, y_buf, x1, gates, lg, lb)


def _moe(x1, idx, gates, pos, counts, wgu, bgu, wd, bd, layer, lg, lb, alpha):
    T, D = x1.shape
    n_assign = T * TOP_K
    counts = counts.reshape(N_EXPERTS)
    padded = (counts + EXPERT_ROWS - 1) // EXPERT_ROWS * EXPERT_ROWS
    group_end = jnp.cumsum(padded)
    group_start = group_end - padded
    experts = jnp.arange(N_EXPERTS, dtype=jnp.int32)
    start_of = jnp.sum(jnp.where(idx[:, :TOP_K, None] == experts, group_start, 0), axis=-1)
    dest = (start_of + pos[:, :TOP_K]).reshape(n_assign).astype(jnp.int32)
    n_blocks = -(-n_assign // EXPERT_ROWS) + N_EXPERTS
    block_row = jnp.arange(n_blocks, dtype=jnp.int32)[:, None] * EXPERT_ROWS
    blk_expert = jnp.minimum(jnp.sum(group_end[None, :] <= block_row, axis=-1),
                             N_EXPERTS - 1).astype(jnp.int32)
    n_used = (group_end[-1:] // EXPERT_ROWS).astype(jnp.int32)
    x_buf = _dispatch(x1, dest, (group_start + counts).astype(jnp.int32),
                      (padded - counts).astype(jnp.int32), n_used, n_blocks * EXPERT_ROWS)
    y_buf = _expert_ffn(x_buf, blk_expert, n_used, wgu, bgu, wd, bd, layer)
    return _combine(y_buf, dest, x1, gates, lg, lb, alpha)


def kernel(x, w_in, g_mix_a, g_mix_b, w_out, ln1_g, ln1_b, w_router, b_router,
           w_gate_up, b_gate_up, w_down, b_down, ln2_g, ln2_b):
    B, S, D = x.shape
    depth = w_in.shape[0]
    wa = g_mix_a.shape[1]
    wb = g_mix_b.shape[1]
    heads_a = wa // HEAD_DIM
    heads_b = wb // HEAD_DIM
    alpha = (2.0 * depth) ** 0.25
    slopes = jnp.asarray([2.0 ** (-8.0 * (h + 1) / heads_a) for h in range(heads_a)], f32)
    row = lambda a: a.reshape(1, -1)

    assert wa == wb, "the q/k/v projection is split into two equal column halves"
    w_in_b = w_in.astype(bf16)
    w_out_b = w_out.astype(bf16)

    xt = x.reshape(B * S, D)
    for l in range(depth):
        ha = _proj(xt, w_in_b, l, 0, f32).reshape(B, S, 3 * wa)
        hb = _proj(xt, w_in_b, l, 1, bf16).reshape(B, S, 3 * wb)
        oa = _dilated_attention(ha, slopes, heads_a).reshape(B * S, wa)
        ob = _stick_attention(hb, heads_b).reshape(B * S, wb)
        x1, idx, gates, pos, counts = _mix_router(
            oa, ob, xt, w_out_b, l, row(g_mix_a[l]), row(g_mix_b[l]),
            row(ln1_g[l]), row(ln1_b[l]), w_router[l], row(b_router[l]), alpha)
        xt = _moe(x1, idx, gates, pos, counts, w_gate_up, b_gate_up, w_down, b_down, l,
                  row(ln2_g[l]), row(ln2_b[l]), alpha)
    return xt.reshape(B, S, D)
```

```python
import functools
import math

import jax
import jax.numpy as jnp
from jax import lax
from jax.experimental import pallas as pl
from jax.experimental.pallas import tpu as pltpu

HEAD_DIM = 64
HEADS_PER_BLOCK = 2
LANES = HEADS_PER_BLOCK * HEAD_DIM
Q_BLOCK = 128
DILATED_PATTERNS = ((128, 1), (512, 4), (2048, 16))
N_EXPERTS = 32
TOP_K = 4
SWIGLU_LIMIT = 7.0
SWIGLU_ALPHA = 1.702
LN_EPS = 1e-5
RMS_EPS = 1e-6
MASKED = -1e30

VMEM_LIMIT = 56 * 1024 * 1024
PROJ_ROWS = 256
ROUTER_ROWS = 256
EXPERT_ROWS = 256
WEIGHT_CAST_ROWS = 256
COMBINE_ROWS = 128
STICK_TILE = 256
LOG2_E = 1.4426950408889634
QUERY_SCALE = LOG2_E / math.sqrt(HEAD_DIM)
STICK_UNDERFLOW_LOG2 = -176.0

f32 = jnp.float32
bf16 = jnp.bfloat16


def _params(n_grid_dims):
    return pltpu.CompilerParams(
        dimension_semantics=("arbitrary",) * n_grid_dims, vmem_limit_bytes=VMEM_LIMIT)


def _dot_nt(a, b):
    return lax.dot_general(a, b, (((1,), (1,)), ((), ())), preferred_element_type=f32)


def _dot(a, b):
    return jnp.dot(a, b, preferred_element_type=f32)


def _proj_kernel(n_query_cols, x_ref, w_ref, o_ref):
    h = _dot(x_ref[...].astype(bf16), w_ref[...])
    query = lax.broadcasted_iota(jnp.int32, (1, h.shape[1]), 1) < n_query_cols
    o_ref[...] = (h * jnp.where(query, QUERY_SCALE, 1.0)).astype(o_ref.dtype)


def _proj(x, w, layer, half, out_dtype):
    T, K = x.shape
    N = w.shape[2] // 2
    tm = min(PROJ_ROWS, T)
    return pl.pallas_call(
        functools.partial(_proj_kernel, N // 3),
        name="proj",
        grid=(T // tm,),
        in_specs=[pl.BlockSpec((tm, K), lambda i: (i, 0)),
                  pl.BlockSpec((None, K, N), lambda i: (layer, 0, half))],
        out_specs=pl.BlockSpec((tm, N), lambda i: (i, 0)),
        out_shape=jax.ShapeDtypeStruct((T, N), out_dtype),
        compiler_params=_params(1),
    )(x, w)


def _dilated_kernel(slopes_ref, q_ref, k_ref, v_ref, o_ref, acc_o, acc_m, acc_l, bias_ref,
                    s0_ref, s1_ref):
    S = q_ref.shape[0]
    QB = Q_BLOCK
    hp = pl.program_id(1)
    head0 = lax.broadcasted_iota(jnp.int32, (QB, LANES), 1) < HEAD_DIM

    def stack_heads(x):
        first = lax.broadcasted_iota(jnp.int32, x.shape, 1) < HEAD_DIM
        return jnp.concatenate([jnp.where(first, x, jnp.zeros_like(x)),
                                jnp.where(first, jnp.zeros_like(x), x)], axis=0)

    for r, (window, d) in enumerate(DILATED_PATTERNS):
        assert window // d == QB
        nblk = (S // d) // QB
        nk = 2 * QB if nblk > 1 else QB
        srow = lax.broadcasted_iota(jnp.int32, (2 * QB, nk), 0)
        scol = lax.broadcasted_iota(jnp.int32, (2 * QB, nk), 1)
        second = srow >= QB
        coef = jnp.where(second, -(slopes_ref[hp * HEADS_PER_BLOCK + 1] * float(d)),
                         -(slopes_ref[hp * HEADS_PER_BLOCK] * float(d)))
        for variant, offset in enumerate((nk - QB, 0)):
            delta = jnp.where(second, srow - QB, srow) - scol + offset
            ok = jnp.logical_and(delta >= 0, delta <= QB)
            bias_ref[variant, :, :nk] = jnp.where(ok, (coef * delta.astype(f32)) * LOG2_E, MASKED)

        n_steps = d * nblk

        def block_rows(idx, d=d, nblk=nblk, nk=nk):
            c = idx // nblk
            nb = idx % nblk
            strided = lambda start, n: pl.ds(start, n, stride=d) if d > 1 else pl.ds(start, n)
            rows = strided(nb * (QB * d) + c, QB)
            krows = strided(jnp.maximum(nb - (nk // QB - 1), 0) * (QB * d) + c, nk)
            variant = jnp.where(nb > 0, 0, 1) if nblk > 1 else 0
            return rows, krows, variant

        def scores(idx, s_ref, nk=nk):
            rows, krows, variant = block_rows(idx)
            qs = stack_heads(q_ref[rows, :].astype(bf16))
            k = k_ref[krows, :].astype(bf16)
            for h in range(HEADS_PER_BLOCK):
                hrows = slice(h * QB, (h + 1) * QB)
                s_ref[hrows, :nk] = _dot_nt(qs[hrows], k) + bias_ref[variant, hrows, :nk]

        def softmax_pv(idx, s_ref, r=r, nk=nk):
            rows, krows, _ = block_rows(idx)
            s = s_ref[:, :nk]
            m = jnp.max(s, axis=-1, keepdims=True)
            p = jnp.exp2(s - m)
            l = jnp.sum(p, axis=-1, keepdims=True)
            p = p.astype(bf16)
            vs = stack_heads(v_ref[krows, :].astype(bf16))
            acc_o[r, rows, :] = _dot(p[:QB], vs[:nk]) + _dot(p[QB:], vs[nk:])
            acc_m[r, rows, :] = jnp.where(head0, m[:QB], m[QB:])
            acc_l[r, rows, :] = jnp.where(head0, l[:QB], l[QB:])

        assert n_steps % 2 == 0
        scores(0, s0_ref)

        def step_pair(j, carry, n_steps=n_steps):
            scores(2 * j + 1, s1_ref)
            softmax_pv(2 * j, s0_ref)
            scores(jnp.minimum(2 * j + 2, n_steps - 1), s0_ref)
            softmax_pv(2 * j + 1, s1_ref)
            return carry

        lax.fori_loop(0, n_steps // 2, step_pair, 0, unroll=4)

    def merge(i, carry):
        rows = pl.ds(pl.multiple_of(i * Q_BLOCK, Q_BLOCK), Q_BLOCK)
        ms = [acc_m[r, rows, :] for r in range(len(DILATED_PATTERNS))]
        m_all = functools.reduce(jnp.maximum, ms)
        num = jnp.zeros((Q_BLOCK, LANES), f32)
        den = jnp.zeros((Q_BLOCK, LANES), f32)
        for r in range(len(DILATED_PATTERNS)):
            w = jnp.exp2(ms[r] - m_all)
            num = num + w * acc_o[r, rows, :]
            den = den + w * acc_l[r, rows, :]
        o_ref[rows, :] = num / den
        return carry

    lax.fori_loop(0, S // Q_BLOCK, merge, 0)


def _dilated_attention(ha, slopes, n_heads):
    B, S, _ = ha.shape
    nhp = n_heads // HEADS_PER_BLOCK
    blk = lambda off: pl.BlockSpec((None, S, LANES), lambda b, h, sl, off=off: (b, 0, off + h))
    return pl.pallas_call(
        _dilated_kernel,
        name="dilated_attn",
        grid_spec=pltpu.PrefetchScalarGridSpec(
            num_scalar_prefetch=1,
            grid=(B, nhp),
            in_specs=[blk(0), blk(nhp), blk(2 * nhp)],
            out_specs=pl.BlockSpec((None, S, LANES), lambda b, h, sl: (b, 0, h)),
            scratch_shapes=[pltpu.VMEM((len(DILATED_PATTERNS), S, LANES), f32)] * 3
            + [pltpu.VMEM((2, 2 * Q_BLOCK, 2 * Q_BLOCK), f32)]
            + [pltpu.VMEM((2 * Q_BLOCK, 2 * Q_BLOCK), f32)] * 2),
        out_shape=jax.ShapeDtypeStruct((B, S, n_heads * HEAD_DIM), f32),
        compiler_params=_params(2),
    )(slopes, ha, ha, ha)


def _stick_kernel(q_ref, k_ref, v_ref, o_ref, later_ref):
    S = q_ref.shape[0]
    T = min(STICK_TILE, S)
    lane = lax.broadcasted_iota(jnp.int32, (T, LANES), 1)
    head0 = lane < HEAD_DIM
    krow = lax.broadcasted_iota(jnp.int32, (T, T), 0)
    kcol = lax.broadcasted_iota(jnp.int32, (T, T), 1)
    later_ref[...] = (krow > kcol).astype(bf16)
    causal = kcol < krow

    def stack_heads(x):
        return jnp.concatenate([jnp.where(head0, x, jnp.zeros_like(x)),
                                jnp.where(head0, jnp.zeros_like(x), x)], axis=0)

    def tile(qs, kb, nearer, acc, diagonal):
        krows = pl.ds(pl.multiple_of(kb * T, T), T)
        k = k_ref[krows, :]
        vs = stack_heads(v_ref[krows, :])
        zs = [_dot_nt(qs[h * T:(h + 1) * T], k) for h in range(HEADS_PER_BLOCK)]
        log_bs, withins, new_nearer = [], [], []
        for z in zs:
            log_b = jnp.minimum(z, 0.0) - jnp.log2(1.0 + jnp.exp2(-jnp.abs(z)))
            log_1mb = log_b - z
            if diagonal:
                log_1mb = jnp.where(causal, log_1mb, 0.0)
            log_bs.append(log_b)
            withins.append(_dot(log_1mb.astype(bf16), later_ref[...]))
            new_nearer.append(nearer[len(log_bs) * T - T:len(log_bs) * T]
                              + jnp.sum(log_1mb, axis=-1, keepdims=True))
        new_nearer = jnp.concatenate(new_nearer, axis=0)
        live = jnp.max(new_nearer) > STICK_UNDERFLOW_LOG2
        for h in range(HEADS_PER_BLOCK):
            a = jnp.exp2(log_bs[h] + withins[h] + nearer[h * T:(h + 1) * T])
            if diagonal:
                a = jnp.where(causal, a, 0.0)
            acc = acc + _dot(a.astype(bf16), vs[h * T:(h + 1) * T])
        return live, new_nearer, acc

    def q_tile(n, has_previous):
        qrows = pl.ds(pl.multiple_of(n * T, T), T)
        qs = stack_heads(q_ref[qrows, :])
        state = tile(qs, n, jnp.zeros((2 * T, 1), f32), jnp.zeros((T, LANES), f32), True)
        walked = 0
        if has_previous:
            state = tile(qs, n - 1, state[1], state[2], False)
            walked = 1

        def more(state):
            i, live, _, _ = state
            return jnp.logical_and(i < n, live)

        def farther(state):
            i, _, nearer, acc = state
            return (i + 1,) + tile(qs, n - 1 - i, nearer, acc, False)

        _, _, _, acc = lax.while_loop(more, farther, (jnp.int32(walked),) + state)
        o_ref[qrows, :] = acc

    q_tile(0, False)

    def later_q_tile(n, carry):
        q_tile(n, True)
        return carry

    lax.fori_loop(1, S // T, later_q_tile, 0)


def _stick_attention(hb, n_heads):
    B, S, _ = hb.shape
    nhp = n_heads // HEADS_PER_BLOCK
    T = min(STICK_TILE, S)
    blk = lambda off: pl.BlockSpec((None, S, LANES), lambda b, h, off=off: (b, 0, off + h))
    return pl.pallas_call(
        _stick_kernel,
        name="stick_attn",
        grid=(B, nhp),
        in_specs=[blk(0), blk(nhp), blk(2 * nhp)],
        out_specs=pl.BlockSpec((None, S, LANES), lambda b, h: (b, 0, h)),
        out_shape=jax.ShapeDtypeStruct((B, S, n_heads * HEAD_DIM), f32),
        scratch_shapes=[pltpu.VMEM((T, T), bf16)],
        compiler_params=_params(2),
    )(hb, hb, hb)


def _layer_norm(u, g, b):
    mu = jnp.mean(u, axis=-1, keepdims=True)
    var = jnp.mean(jnp.square(u - mu), axis=-1, keepdims=True)
    return (u - mu) * lax.rsqrt(var + LN_EPS) * g + b


def _rms_norm(o, g):
    return o * lax.rsqrt(jnp.mean(jnp.square(o), axis=-1, keepdims=True) + RMS_EPS) * g


def _mix_router_kernel(alpha, oa_ref, ob_ref, x_ref, wo_ref, ga_ref, gb_ref, lg_ref, lb_ref,
                       wr_ref, br_ref, x1_ref, idx_ref, gate_ref, pos_ref, cnt_ref, running,
                       wr_hi, wr_lo):
    tm = x_ref.shape[0]
    wa = oa_ref.shape[1]

    @pl.when(pl.program_id(0) == 0)
    def _():
        running[...] = jnp.zeros_like(running)
        w_hi = wr_ref[...].astype(bf16)
        wr_hi[...] = w_hi
        wr_lo[...] = (wr_ref[...] - w_hi.astype(f32)).astype(bf16)

    mix_a = _rms_norm(oa_ref[...], ga_ref[...]).astype(bf16)
    mix_b = _rms_norm(ob_ref[...], gb_ref[...]).astype(bf16)
    y = _dot(mix_a, wo_ref[:wa, :]) + _dot(mix_b, wo_ref[wa:, :])
    x1 = _layer_norm(alpha * x_ref[...] + y, lg_ref[...], lb_ref[...])
    x1_ref[...] = x1

    x_hi = x1.astype(bf16)
    x_lo = (x1 - x_hi.astype(f32)).astype(bf16)
    logits = (_dot(x_hi, wr_hi[...]) + (_dot(x_hi, wr_lo[...]) + _dot(x_lo, wr_hi[...]))) + br_ref[...]
    lane = lax.broadcasted_iota(jnp.int32, (tm, N_EXPERTS), 1)
    work = logits
    vals, sels, idxs = [], [], []
    for _ in range(TOP_K):
        m = jnp.max(work, axis=-1, keepdims=True)
        idx = jnp.min(jnp.where(work == m, lane, N_EXPERTS), axis=-1, keepdims=True)
        sel = lane == idx
        work = jnp.where(sel, -jnp.inf, work)
        vals.append(m)
        sels.append(sel)
        idxs.append(idx)
    exps = [jnp.exp(v - vals[0]) for v in vals]
    denom = functools.reduce(jnp.add, exps)

    chosen = functools.reduce(jnp.logical_or, sels)
    trow = lax.broadcasted_iota(jnp.int32, (tm, tm), 0)
    tcol = lax.broadcasted_iota(jnp.int32, (tm, tm), 1)
    earlier = (tcol < trow).astype(bf16)
    rank = _dot(earlier, chosen.astype(bf16)) + running[...]
    running[...] = running[...] + jnp.sum(chosen.astype(f32), axis=0, keepdims=True)
    cnt_ref[...] = running[...].astype(jnp.int32)

    out_lane = lax.broadcasted_iota(jnp.int32, (tm, LANES), 1)
    idx_out = jnp.zeros((tm, LANES), jnp.int32)
    gate_out = jnp.zeros((tm, LANES), f32)
    pos_out = jnp.zeros((tm, LANES), jnp.int32)
    for k in range(TOP_K):
        pos = jnp.sum(jnp.where(sels[k], rank, 0.0), axis=-1, keepdims=True).astype(jnp.int32)
        idx_out = jnp.where(out_lane == k, idxs[k], idx_out)
        gate_out = jnp.where(out_lane == k, exps[k] / denom, gate_out)
        pos_out = jnp.where(out_lane == k, pos, pos_out)
    idx_ref[...] = idx_out
    gate_ref[...] = gate_out
    pos_ref[...] = pos_out


def _mix_router(oa, ob, x, wo, layer, ga, gb, lg, lb, wr, br, alpha):
    T, D = x.shape
    W = oa.shape[1]
    tm = min(ROUTER_ROWS, T)
    rows = lambda n: pl.BlockSpec((tm, n), lambda i: (i, 0))
    full = lambda a: pl.BlockSpec(a.shape, lambda i: (0,) * a.ndim)
    return pl.pallas_call(
        functools.partial(_mix_router_kernel, alpha),
        name="mix_router",
        grid=(T // tm,),
        in_specs=[rows(W), rows(W), rows(D), pl.BlockSpec((None, D, D), lambda i: (layer, 0, 0)),
                  full(ga), full(gb), full(lg), full(lb), full(wr), full(br)],
        out_specs=[rows(D), rows(LANES), rows(LANES), rows(LANES),
                   pl.BlockSpec((1, N_EXPERTS), lambda i: (0, 0))],
        out_shape=[jax.ShapeDtypeStruct((T, D), f32),
                   jax.ShapeDtypeStruct((T, LANES), jnp.int32),
                   jax.ShapeDtypeStruct((T, LANES), f32),
                   jax.ShapeDtypeStruct((T, LANES), jnp.int32),
                   jax.ShapeDtypeStruct((1, N_EXPERTS), jnp.int32)],
        scratch_shapes=[pltpu.VMEM((1, N_EXPERTS), f32),
                        pltpu.VMEM(wr.shape, bf16), pltpu.VMEM(wr.shape, bf16)],
        compiler_params=_params(1),
    )(oa, ob, x, wo, ga, gb, lg, lb, wr, br)


def _clamped_swiglu(h):
    d_ff = h.shape[1] // 2
    gate = jnp.minimum(h[:, :d_ff], SWIGLU_LIMIT)
    up = jnp.clip(h[:, d_ff:], -SWIGLU_LIMIT, SWIGLU_LIMIT)
    return (up + 1.0) * (gate * jax.nn.sigmoid(gate * SWIGLU_ALPHA))


def _invert_kernel(dest_ref, row_id_ref):
    def clear(r, carry):
        row_id_ref[r] = -1
        return carry
    lax.fori_loop(0, row_id_ref.shape[0], clear, 0, unroll=8)

    def place(a, carry):
        row_id_ref[dest_ref[a]] = a
        return carry
    lax.fori_loop(0, dest_ref.shape[0], place, 0, unroll=8)


def _invert(dest_flat, n_rows):
    return pl.pallas_call(
        _invert_kernel,
        name="moe_invert",
        grid_spec=pltpu.PrefetchScalarGridSpec(
            num_scalar_prefetch=1, grid=(1,), in_specs=[],
            out_specs=pl.BlockSpec(memory_space=pltpu.SMEM)),
        out_shape=jax.ShapeDtypeStruct((n_rows,), jnp.int32),
        compiler_params=_params(1),
    )(dest_flat)


def _round_weight_on_new_expert(i, used, blk_expert_ref, w_ref, wq_ref):
    first_of_expert = jnp.logical_or(
        i == 0, blk_expert_ref[i] != blk_expert_ref[jnp.maximum(i - 1, 0)])

    @pl.when(jnp.logical_and(used, first_of_expert))
    def _():
        chunk = math.gcd(WEIGHT_CAST_ROWS, w_ref.shape[0])

        def cast_rows(c, carry):
            rows = pl.ds(pl.multiple_of(c * chunk, chunk), chunk)
            wq_ref[rows, :] = w_ref[rows, :].astype(bf16)
            return carry
        lax.fori_loop(0, w_ref.shape[0] // chunk, cast_rows, 0)


def _gate_up_kernel(blk_expert_ref, n_used_ref, row_id_ref, x_hbm, w_ref, b_ref, o_ref,
                    wq_ref, rows_even, rows_odd, sems):
    i = pl.program_id(0)
    n_used = n_used_ref[0]
    used = i < n_used
    parity = i % 2
    rows_bufs = (rows_even, rows_odd)

    def row_copy(block, j, p):
        token = jnp.maximum(row_id_ref[block * EXPERT_ROWS + j], 0) // TOP_K
        return pltpu.make_async_copy(
            x_hbm.at[pl.ds(token, 1), :], rows_bufs[p].at[pl.ds(j, 1), :], sems.at[p])

    def wait_rows(p):
        pltpu.make_async_copy(
            x_hbm.at[pl.ds(0, EXPERT_ROWS), :], rows_bufs[p], sems.at[p]).wait()

    @pl.when(i == 0)
    def _():
        def start(j, carry):
            row_copy(0, j, 0).start()
            return carry
        lax.fori_loop(0, EXPERT_ROWS, start, 0, unroll=8)

    _round_weight_on_new_expert(i, used, blk_expert_ref, w_ref, wq_ref)

    for p in range(2):
        @pl.when(jnp.logical_and(used, parity == p))
        def _(p=p):
            wait_rows(p)
            following = jnp.minimum(i + 1, n_used - 1)
            for j in range(EXPERT_ROWS):
                row_copy(following, j, 1 - p).start()
            h = _dot(rows_bufs[p][...].astype(bf16), wq_ref[...]) + b_ref[...]
            o_ref[...] = _clamped_swiglu(h).astype(o_ref.dtype)

        @pl.when(jnp.logical_and(i == n_used, parity == p))
        def _(p=p):
            wait_rows(p)

    @pl.when(jnp.logical_not(used))
    def _():
        o_ref[...] = jnp.zeros_like(o_ref)


def _gate_up(x, row_id, blk_expert, n_used, w, b, layer):
    T, D = x.shape
    L, E, _, N = w.shape
    R = row_id.shape[0]
    expert = lambda i, be, nu, rid: (layer, be[i], 0, 0)
    return pl.pallas_call(
        _gate_up_kernel,
        name="expert_gate_up",
        grid_spec=pltpu.PrefetchScalarGridSpec(
            num_scalar_prefetch=3,
            grid=(R // EXPERT_ROWS,),
            in_specs=[pl.BlockSpec(memory_space=pl.ANY),
                      pl.BlockSpec((None, None, D, N), expert),
                      pl.BlockSpec((None, None, 1, N), expert)],
            out_specs=pl.BlockSpec((EXPERT_ROWS, N // 2), lambda i, be, nu, rid: (i, 0)),
            scratch_shapes=[pltpu.VMEM((D, N), bf16),
                            pltpu.VMEM((EXPERT_ROWS, D), f32), pltpu.VMEM((EXPERT_ROWS, D), f32),
                            pltpu.SemaphoreType.DMA((2,))]),
        out_shape=jax.ShapeDtypeStruct((R, N // 2), bf16),
        compiler_params=_params(1),
    )(blk_expert, n_used, row_id, x, w, b.reshape(L, E, 1, N))


def _down_kernel(n_assign, blk_expert_ref, n_used_ref, row_id_ref, a_ref, w_ref, b_ref, y_hbm,
                 wq_ref, rows_even, rows_odd, sems):
    i = pl.program_id(0)
    n_used = n_used_ref[0]
    used = i < n_used
    parity = i % 2
    rows_bufs = (rows_even, rows_odd)

    def row_copy(block, j, p):
        row = block * EXPERT_ROWS + j
        assignment = row_id_ref[row]
        target = jnp.where(assignment >= 0, assignment, n_assign + row)
        return pltpu.make_async_copy(
            rows_bufs[p].at[pl.ds(j, 1), :], y_hbm.at[pl.ds(target, 1), :], sems.at[p])

    def wait_rows(p):
        pltpu.make_async_copy(
            rows_bufs[p], y_hbm.at[pl.ds(0, EXPERT_ROWS), :], sems.at[p]).wait()

    def multiply(p):
        rows_bufs[p][...] = _dot(a_ref[...], wq_ref[...]) + b_ref[...]

    _round_weight_on_new_expert(i, used, blk_expert_ref, w_ref, wq_ref)

    @pl.when(i == 0)
    def _():
        multiply(0)

    for p in range(2):
        @pl.when(jnp.logical_and(jnp.logical_and(used, i >= 2), parity == p))
        def _(p=p):
            wait_rows(p)

        @pl.when(jnp.logical_and(jnp.logical_and(used, i >= 1), parity == p))
        def _(p=p):
            for j in range(EXPERT_ROWS):
                row_copy(i - 1, j, 1 - p).start()
            multiply(p)

        @pl.when(jnp.logical_and(i == n_used, parity == p))
        def _(p=p):
            def start(j, carry):
                row_copy(n_used - 1, j, 1 - p).start()
                return carry
            lax.fori_loop(0, EXPERT_ROWS, start, 0, unroll=8)

            @pl.when(n_used >= 2)
            def _():
                wait_rows(p)
            wait_rows(1 - p)


def _down(act, row_id, blk_expert, n_used, w, b, layer, n_assign):
    R, F = act.shape
    L, E, _, D = w.shape
    used = lambda i, be, nu, rid: (jnp.minimum(i, nu[0] - 1), 0)
    expert = lambda i, be, nu, rid: (layer, be[i], 0, 0)
    return pl.pallas_call(
        functools.partial(_down_kernel, n_assign),
        name="expert_down",
        grid_spec=pltpu.PrefetchScalarGridSpec(
            num_scalar_prefetch=3,
            grid=(R // EXPERT_ROWS,),
            in_specs=[pl.BlockSpec((EXPERT_ROWS, F), used),
                      pl.BlockSpec((None, None, F, D), expert),
                      pl.BlockSpec((None, None, 1, D), expert)],
            out_specs=pl.BlockSpec(memory_space=pl.ANY),
            scratch_shapes=[pltpu.VMEM((F, D), bf16),
                            pltpu.VMEM((EXPERT_ROWS, D), f32), pltpu.VMEM((EXPERT_ROWS, D), f32),
                            pltpu.SemaphoreType.DMA((2,))]),
        out_shape=jax.ShapeDtypeStruct((n_assign + R, D), f32),
        compiler_params=_params(1),
    )(blk_expert, n_used, row_id, act, w, b.reshape(L, E, 1, D))


def _combine_dense_kernel(alpha, y_ref, x_ref, gate_ref, lg_ref, lb_ref, o_ref):
    D = x_ref.shape[1]
    y = jnp.zeros(x_ref.shape, f32)
    for k in range(TOP_K):
        y = y + gate_ref[:, k:k + 1] * y_ref[:, k * D:(k + 1) * D]
    o_ref[...] = _layer_norm(alpha * x_ref[...] + y, lg_ref[...], lb_ref[...])


def _combine_dense(y, x1, gates, lg, lb, alpha):
    T, D = x1.shape
    tm = min(COMBINE_ROWS, T)
    y = y.reshape(y.shape[0] // TOP_K, TOP_K * D)
    return pl.pallas_call(
        functools.partial(_combine_dense_kernel, alpha),
        name="moe_combine",
        grid=(T // tm,),
        in_specs=[pl.BlockSpec((tm, TOP_K * D), lambda i: (i, 0)),
                  pl.BlockSpec((tm, D), lambda i: (i, 0)),
                  pl.BlockSpec((tm, LANES), lambda i: (i, 0)),
                  pl.BlockSpec((1, D), lambda i: (0, 0)),
                  pl.BlockSpec((1, D), lambda i: (0, 0))],
        out_specs=pl.BlockSpec((tm, D), lambda i: (i, 0)),
        out_shape=jax.ShapeDtypeStruct((T, D), f32),
        compiler_params=_params(1),
    )(y, x1, gates, lg, lb)


def _moe(x1, idx, gates, pos, counts, wgu, bgu, wd, bd, layer, lg, lb, alpha):
    T, D = x1.shape
    n_assign = T * TOP_K
    counts = counts.reshape(N_EXPERTS)
    padded = (counts + EXPERT_ROWS - 1) // EXPERT_ROWS * EXPERT_ROWS
    group_end = jnp.cumsum(padded)
    group_start = group_end - padded
    experts = jnp.arange(N_EXPERTS, dtype=jnp.int32)
    start_of = jnp.sum(jnp.where(idx[:, :TOP_K, None] == experts, group_start, 0), axis=-1)
    dest = (start_of + pos[:, :TOP_K]).reshape(n_assign).astype(jnp.int32)
    n_blocks = -(-n_assign // EXPERT_ROWS) + N_EXPERTS
    block_row = jnp.arange(n_blocks, dtype=jnp.int32)[:, None] * EXPERT_ROWS
    blk_expert = jnp.minimum(jnp.sum(group_end[None, :] <= block_row, axis=-1),
                             N_EXPERTS - 1).astype(jnp.int32)
    n_used = (group_end[-1:] // EXPERT_ROWS).astype(jnp.int32)
    row_id = _invert(dest, n_blocks * EXPERT_ROWS)
    act = _gate_up(x1, row_id, blk_expert, n_used, wgu, bgu, layer)
    y = _down(act, row_id, blk_expert, n_used, wd, bd, layer, n_assign)
    return _combine_dense(y, x1, gates, lg, lb, alpha)


def kernel(x, w_in, g_mix_a, g_mix_b, w_out, ln1_g, ln1_b, w_router, b_router,
           w_gate_up, b_gate_up, w_down, b_down, ln2_g, ln2_b):
    B, S, D = x.shape
    depth = w_in.shape[0]
    wa = g_mix_a.shape[1]
    wb = g_mix_b.shape[1]
    heads_a = wa // HEAD_DIM
    heads_b = wb // HEAD_DIM
    alpha = (2.0 * depth) ** 0.25
    slopes = jnp.asarray([2.0 ** (-8.0 * (h + 1) / heads_a) for h in range(heads_a)], f32)
    row = lambda a: a.reshape(1, -1)

    assert wa == wb, "the q/k/v projection is split into two equal column halves"
    w_in_b = w_in.astype(bf16)
    w_out_b = w_out.astype(bf16)

    xt = x.reshape(B * S, D)
    for l in range(depth):
        ha = _proj(xt, w_in_b, l, 0, f32).reshape(B, S, 3 * wa)
        hb = _proj(xt, w_in_b, l, 1, bf16).reshape(B, S, 3 * wb)
        oa = _dilated_attention(ha, slopes, heads_a).reshape(B * S, wa)
        ob = _stick_attention(hb, heads_b).reshape(B * S, wb)
        x1, idx, gates, pos, counts = _mix_router(
            oa, ob, xt, w_out_b, l, row(g_mix_a[l]), row(g_mix_b[l]),
            row(ln1_g[l]), row(ln1_b[l]), w_router[l], row(b_router[l]), alpha)
        xt = _moe(x1, idx, gates, pos, counts, w_gate_up, b_gate_up, w_down, b_down, l,
                  row(ln2_g[l]), row(ln2_b[l]), alpha)
    return xt.reshape(B, S, D)
```

```python
import functools
import math

import jax
import jax.numpy as jnp
from jax import lax
from jax.experimental import pallas as pl
from jax.experimental.pallas import tpu as pltpu

HEAD_DIM = 64
HEADS_PER_BLOCK = 2
LANES = HEADS_PER_BLOCK * HEAD_DIM
Q_BLOCK = 128
DILATED_PATTERNS = ((128, 1), (512, 4), (2048, 16))
N_EXPERTS = 32
TOP_K = 4
SWIGLU_LIMIT = 7.0
SWIGLU_ALPHA = 1.702
LN_EPS = 1e-5
RMS_EPS = 1e-6
MASKED = -1e30

VMEM_LIMIT = 56 * 1024 * 1024
PROJ_ROWS = 256
ROUTER_ROWS = 512
DISPATCH_ROWS = 128
EXPERT_ROWS = 256
WEIGHT_CAST_ROWS = 256
COMBINE_ROWS = 128
STICK_TILE = 256
LOG2_E = 1.4426950408889634
QUERY_SCALE = LOG2_E / math.sqrt(HEAD_DIM)
STICK_UNDERFLOW_LOG2 = -176.0

f32 = jnp.float32
bf16 = jnp.bfloat16


def _params(n_grid_dims):
    return pltpu.CompilerParams(
        dimension_semantics=("arbitrary",) * n_grid_dims, vmem_limit_bytes=VMEM_LIMIT)


def _dot_nt(a, b):
    return lax.dot_general(a, b, (((1,), (1,)), ((), ())), preferred_element_type=f32)


def _dot(a, b):
    return jnp.dot(a, b, preferred_element_type=f32)


def _proj_kernel(n_query_cols, x_ref, w_ref, o_ref):
    h = _dot(x_ref[...].astype(bf16), w_ref[...])
    query = lax.broadcasted_iota(jnp.int32, (1, h.shape[1]), 1) < n_query_cols
    o_ref[...] = (h * jnp.where(query, QUERY_SCALE, 1.0)).astype(o_ref.dtype)


def _proj(x, w, layer, half, out_dtype):
    T, K = x.shape
    N = w.shape[2] // 2
    tm = min(PROJ_ROWS, T)
    return pl.pallas_call(
        functools.partial(_proj_kernel, N // 3),
        name="proj",
        grid=(T // tm,),
        in_specs=[pl.BlockSpec((tm, K), lambda i: (i, 0)),
                  pl.BlockSpec((None, K, N), lambda i: (layer, 0, half))],
        out_specs=pl.BlockSpec((tm, N), lambda i: (i, 0)),
        out_shape=jax.ShapeDtypeStruct((T, N), out_dtype),
        compiler_params=_params(1),
    )(x, w)


def _dilated_kernel(slopes_ref, q_ref, k_ref, v_ref, o_ref, acc_o, acc_m, acc_l, bias_ref,
                    s0_ref, s1_ref):
    S = q_ref.shape[0]
    QB = Q_BLOCK
    hp = pl.program_id(0)
    head0 = lax.broadcasted_iota(jnp.int32, (QB, LANES), 1) < HEAD_DIM

    def stack_heads(x):
        first = lax.broadcasted_iota(jnp.int32, x.shape, 1) < HEAD_DIM
        return jnp.concatenate([jnp.where(first, x, jnp.zeros_like(x)),
                                jnp.where(first, jnp.zeros_like(x), x)], axis=0)

    def keys_per_step(d):
        return 2 * QB if (S // d) // QB > 1 else QB

    @pl.when(pl.program_id(1) == 0)
    def _():
        for r, (window, d) in enumerate(DILATED_PATTERNS):
            nk = keys_per_step(d)
            srow = lax.broadcasted_iota(jnp.int32, (2 * QB, nk), 0)
            scol = lax.broadcasted_iota(jnp.int32, (2 * QB, nk), 1)
            second = srow >= QB
            coef = jnp.where(second, -(slopes_ref[hp * HEADS_PER_BLOCK + 1] * float(d)),
                             -(slopes_ref[hp * HEADS_PER_BLOCK] * float(d)))
            for variant, offset in enumerate((nk - QB, 0)):
                delta = jnp.where(second, srow - QB, srow) - scol + offset
                ok = jnp.logical_and(delta >= 0, delta <= QB)
                bias_ref[r, variant, :, :nk] = jnp.where(
                    ok, (coef * delta.astype(f32)) * LOG2_E, MASKED)

    for r, (window, d) in enumerate(DILATED_PATTERNS):
        assert window // d == QB
        nblk = (S // d) // QB
        nk = keys_per_step(d)
        n_steps = d * nblk

        def block_rows(idx, d=d, nblk=nblk, nk=nk):
            c = idx // nblk
            nb = idx % nblk
            strided = lambda start, n: pl.ds(start, n, stride=d) if d > 1 else pl.ds(start, n)
            rows = strided(nb * (QB * d) + c, QB)
            krows = strided(jnp.maximum(nb - (nk // QB - 1), 0) * (QB * d) + c, nk)
            variant = jnp.where(nb > 0, 0, 1) if nblk > 1 else 0
            return rows, krows, variant

        def scores(idx, s_ref, r=r, nk=nk):
            rows, krows, variant = block_rows(idx)
            qs = stack_heads(q_ref[rows, :].astype(bf16))
            k = k_ref[krows, :].astype(bf16)
            for h in range(HEADS_PER_BLOCK):
                hrows = slice(h * QB, (h + 1) * QB)
                s_ref[hrows, :nk] = _dot_nt(qs[hrows], k) + bias_ref[r, variant, hrows, :nk]

        def softmax_pv(idx, s_ref, r=r, nk=nk):
            rows, krows, _ = block_rows(idx)
            s = s_ref[:, :nk]
            m = jnp.max(s, axis=-1, keepdims=True)
            p = jnp.exp2(s - m)
            l = jnp.sum(p, axis=-1, keepdims=True)
            p = p.astype(bf16)
            vs = stack_heads(v_ref[krows, :].astype(bf16))
            acc_o[r, rows, :] = _dot(p[:QB], vs[:nk]) + _dot(p[QB:], vs[nk:])
            acc_m[r, rows, :] = jnp.where(head0, m[:QB], m[QB:])
            acc_l[r, rows, :] = jnp.where(head0, l[:QB], l[QB:])

        assert n_steps % 2 == 0
        scores(0, s0_ref)

        def step_pair(j, carry, n_steps=n_steps):
            scores(2 * j + 1, s1_ref)
            softmax_pv(2 * j, s0_ref)
            scores(jnp.minimum(2 * j + 2, n_steps - 1), s0_ref)
            softmax_pv(2 * j + 1, s1_ref)
            return carry

        lax.fori_loop(0, n_steps // 2, step_pair, 0, unroll=4)

    def merge(i, carry):
        rows = pl.ds(pl.multiple_of(i * Q_BLOCK, Q_BLOCK), Q_BLOCK)
        ms = [acc_m[r, rows, :] for r in range(len(DILATED_PATTERNS))]
        m_all = functools.reduce(jnp.maximum, ms)
        num = jnp.zeros((Q_BLOCK, LANES), f32)
        den = jnp.zeros((Q_BLOCK, LANES), f32)
        for r in range(len(DILATED_PATTERNS)):
            w = jnp.exp2(ms[r] - m_all)
            num = num + w * acc_o[r, rows, :]
            den = den + w * acc_l[r, rows, :]
        o_ref[rows, :] = num / den
        return carry

    lax.fori_loop(0, S // Q_BLOCK, merge, 0)


def _dilated_attention(ha, slopes, n_heads):
    B, S, _ = ha.shape
    nhp = n_heads // HEADS_PER_BLOCK
    blk = lambda off: pl.BlockSpec((None, S, LANES), lambda h, b, sl, off=off: (b, 0, off + h))
    return pl.pallas_call(
        _dilated_kernel,
        name="dilated_attn",
        grid_spec=pltpu.PrefetchScalarGridSpec(
            num_scalar_prefetch=1,
            grid=(nhp, B),
            in_specs=[blk(0), blk(nhp), blk(2 * nhp)],
            out_specs=pl.BlockSpec((None, S, LANES), lambda h, b, sl: (b, 0, h)),
            scratch_shapes=[pltpu.VMEM((len(DILATED_PATTERNS), S, LANES), f32)] * 3
            + [pltpu.VMEM((len(DILATED_PATTERNS), 2, 2 * Q_BLOCK, 2 * Q_BLOCK), f32)]
            + [pltpu.VMEM((2 * Q_BLOCK, 2 * Q_BLOCK), f32)] * 2),
        out_shape=jax.ShapeDtypeStruct((B, S, n_heads * HEAD_DIM), f32),
        compiler_params=_params(2),
    )(slopes, ha, ha, ha)


def _stick_kernel(q_ref, k_ref, v_ref, o_ref, later_ref):
    S = q_ref.shape[0]
    T = min(STICK_TILE, S)
    lane = lax.broadcasted_iota(jnp.int32, (T, LANES), 1)
    head0 = lane < HEAD_DIM
    krow = lax.broadcasted_iota(jnp.int32, (T, T), 0)
    kcol = lax.broadcasted_iota(jnp.int32, (T, T), 1)
    later_ref[...] = (krow > kcol).astype(bf16)
    causal = kcol < krow

    def stack_heads(x):
        return jnp.concatenate([jnp.where(head0, x, jnp.zeros_like(x)),
                                jnp.where(head0, jnp.zeros_like(x), x)], axis=0)

    def tile(qs, kb, nearer, acc, diagonal):
        krows = pl.ds(pl.multiple_of(kb * T, T), T)
        k = k_ref[krows, :]
        vs = stack_heads(v_ref[krows, :])
        zs = [_dot_nt(qs[h * T:(h + 1) * T], k) for h in range(HEADS_PER_BLOCK)]
        log_bs, withins, new_nearer = [], [], []
        for z in zs:
            log_b = jnp.minimum(z, 0.0) - jnp.log2(1.0 + jnp.exp2(-jnp.abs(z)))
            log_1mb = log_b - z
            if diagonal:
                log_1mb = jnp.where(causal, log_1mb, 0.0)
            log_bs.append(log_b)
            withins.append(_dot(log_1mb.astype(bf16), later_ref[...]))
            new_nearer.append(nearer[len(log_bs) * T - T:len(log_bs) * T]
                              + jnp.sum(log_1mb, axis=-1, keepdims=True))
        new_nearer = jnp.concatenate(new_nearer, axis=0)
        live = jnp.max(new_nearer) > STICK_UNDERFLOW_LOG2
        for h in range(HEADS_PER_BLOCK):
            a = jnp.exp2(log_bs[h] + withins[h] + nearer[h * T:(h + 1) * T])
            if diagonal:
                a = jnp.where(causal, a, 0.0)
            acc = acc + _dot(a.astype(bf16), vs[h * T:(h + 1) * T])
        return live, new_nearer, acc

    def q_tile(n, has_previous):
        qrows = pl.ds(pl.multiple_of(n * T, T), T)
        qs = stack_heads(q_ref[qrows, :])
        state = tile(qs, n, jnp.zeros((2 * T, 1), f32), jnp.zeros((T, LANES), f32), True)
        walked = 0
        if has_previous:
            state = tile(qs, n - 1, state[1], state[2], False)
            walked = 1

        def more(state):
            i, live, _, _ = state
            return jnp.logical_and(i < n, live)

        def farther(state):
            i, _, nearer, acc = state
            return (i + 1,) + tile(qs, n - 1 - i, nearer, acc, False)

        _, _, _, acc = lax.while_loop(more, farther, (jnp.int32(walked),) + state)
        o_ref[qrows, :] = acc

    q_tile(0, False)

    def later_q_tile(n, carry):
        q_tile(n, True)
        return carry

    lax.fori_loop(1, S // T, later_q_tile, 0)


def _stick_attention(hb, n_heads):
    B, S, _ = hb.shape
    nhp = n_heads // HEADS_PER_BLOCK
    T = min(STICK_TILE, S)
    blk = lambda off: pl.BlockSpec((None, S, LANES), lambda b, h, off=off: (b, 0, off + h))
    return pl.pallas_call(
        _stick_kernel,
        name="stick_attn",
        grid=(B, nhp),
        in_specs=[blk(0), blk(nhp), blk(2 * nhp)],
        out_specs=pl.BlockSpec((None, S, LANES), lambda b, h: (b, 0, h)),
        out_shape=jax.ShapeDtypeStruct((B, S, n_heads * HEAD_DIM), f32),
        scratch_shapes=[pltpu.VMEM((T, T), bf16)],
        compiler_params=_params(2),
    )(hb, hb, hb)


def _layer_norm(u, g, b):
    mu = jnp.mean(u, axis=-1, keepdims=True)
    var = jnp.mean(jnp.square(u - mu), axis=-1, keepdims=True)
    return (u - mu) * lax.rsqrt(var + LN_EPS) * g + b


def _rms_norm(o, g):
    return o * lax.rsqrt(jnp.mean(jnp.square(o), axis=-1, keepdims=True) + RMS_EPS) * g


def _mix_router_kernel(alpha, oa_ref, ob_ref, x_ref, wo_ref, ga_ref, gb_ref, lg_ref, lb_ref,
                       wr_ref, br_ref, x1_ref, idx_ref, gate_ref, pos_ref, cnt_ref, running,
                       wr_hi, wr_lo):
    tm = x_ref.shape[0]
    wa = oa_ref.shape[1]

    @pl.when(pl.program_id(0) == 0)
    def _():
        running[...] = jnp.zeros_like(running)
        w_hi = wr_ref[...].astype(bf16)
        wr_hi[...] = w_hi
        wr_lo[...] = (wr_ref[...] - w_hi.astype(f32)).astype(bf16)

    mix_a = _rms_norm(oa_ref[...], ga_ref[...]).astype(bf16)
    mix_b = _rms_norm(ob_ref[...], gb_ref[...]).astype(bf16)
    y = _dot(mix_a, wo_ref[:wa, :]) + _dot(mix_b, wo_ref[wa:, :])
    x1 = _layer_norm(alpha * x_ref[...] + y, lg_ref[...], lb_ref[...])
    x1_ref[...] = x1

    x_hi = x1.astype(bf16)
    x_lo = (x1 - x_hi.astype(f32)).astype(bf16)
    logits = (_dot(x_hi, wr_hi[...]) + (_dot(x_hi, wr_lo[...]) + _dot(x_lo, wr_hi[...]))) + br_ref[...]
    lane = lax.broadcasted_iota(jnp.int32, (tm, N_EXPERTS), 1)
    work = logits
    vals, sels, idxs = [], [], []
    for _ in range(TOP_K):
        m = jnp.max(work, axis=-1, keepdims=True)
        idx = jnp.min(jnp.where(work == m, lane, N_EXPERTS), axis=-1, keepdims=True)
        sel = lane == idx
        work = jnp.where(sel, -jnp.inf, work)
        vals.append(m)
        sels.append(sel)
        idxs.append(idx)
    exps = [jnp.exp(v - vals[0]) for v in vals]
    denom = functools.reduce(jnp.add, exps)

    chosen = functools.reduce(jnp.logical_or, sels)
    trow = lax.broadcasted_iota(jnp.int32, (tm, tm), 0)
    tcol = lax.broadcasted_iota(jnp.int32, (tm, tm), 1)
    earlier = (tcol < trow).astype(bf16)
    rank = _dot(earlier, chosen.astype(bf16)) + running[...]
    running[...] = running[...] + jnp.sum(chosen.astype(f32), axis=0, keepdims=True)
    cnt_ref[...] = running[...].astype(jnp.int32)

    out_lane = lax.broadcasted_iota(jnp.int32, (tm, LANES), 1)
    idx_out = jnp.zeros((tm, LANES), jnp.int32)
    gate_out = jnp.zeros((tm, LANES), f32)
    pos_out = jnp.zeros((tm, LANES), jnp.int32)
    for k in range(TOP_K):
        pos = jnp.sum(jnp.where(sels[k], rank, 0.0), axis=-1, keepdims=True).astype(jnp.int32)
        idx_out = jnp.where(out_lane == k, idxs[k], idx_out)
        gate_out = jnp.where(out_lane == k, exps[k] / denom, gate_out)
        pos_out = jnp.where(out_lane == k, pos, pos_out)
    idx_ref[...] = idx_out
    gate_ref[...] = gate_out
    pos_ref[...] = pos_out


def _mix_router(oa, ob, x, wo, layer, ga, gb, lg, lb, wr, br, alpha):
    T, D = x.shape
    W = oa.shape[1]
    tm = min(ROUTER_ROWS, T)
    rows = lambda n: pl.BlockSpec((tm, n), lambda i: (i, 0))
    full = lambda a: pl.BlockSpec(a.shape, lambda i: (0,) * a.ndim)
    return pl.pallas_call(
        functools.partial(_mix_router_kernel, alpha),
        name="mix_router",
        grid=(T // tm,),
        in_specs=[rows(W), rows(W), rows(D), pl.BlockSpec((None, D, D), lambda i: (layer, 0, 0)),
                  full(ga), full(gb), full(lg), full(lb), full(wr), full(br)],
        out_specs=[rows(D), rows(LANES), rows(LANES), rows(LANES),
                   pl.BlockSpec((1, N_EXPERTS), lambda i: (0, 0))],
        out_shape=[jax.ShapeDtypeStruct((T, D), f32),
                   jax.ShapeDtypeStruct((T, LANES), jnp.int32),
                   jax.ShapeDtypeStruct((T, LANES), f32),
                   jax.ShapeDtypeStruct((T, LANES), jnp.int32),
                   jax.ShapeDtypeStruct((1, N_EXPERTS), jnp.int32)],
        scratch_shapes=[pltpu.VMEM((1, N_EXPERTS), f32),
                        pltpu.VMEM(wr.shape, bf16), pltpu.VMEM(wr.shape, bf16)],
        compiler_params=_params(1),
    )(oa, ob, x, wo, ga, gb, lg, lb, wr, br)


def _dispatch_kernel(dest_ref, pad_start_ref, pad_len_ref, n_used_ref, x_ref, buf_ref,
                     zeros_ref, sem, zero_sem):
    tm = x_ref.shape[0]
    i = pl.program_id(0)
    base = i * (tm * TOP_K)
    n_blocks = buf_ref.shape[0] // EXPERT_ROWS

    @pl.when(i == 0)
    def _():
        zeros_ref[...] = jnp.zeros_like(zeros_ref)

        def pad_row(e, j):
            return pltpu.make_async_copy(
                zeros_ref.at[pl.ds(0, 1), :], buf_ref.at[pl.ds(pad_start_ref[e] + j, 1), :], zero_sem)

        def unused_block(b):
            rows = pl.ds(pl.multiple_of(b * EXPERT_ROWS, EXPERT_ROWS), EXPERT_ROWS)
            return pltpu.make_async_copy(zeros_ref, buf_ref.at[rows, :], zero_sem)

        def for_each(action):
            def per_expert(e, carry):
                def per_row(j, c):
                    action(pad_row(e, j))
                    return c
                return lax.fori_loop(0, pad_len_ref[e], per_row, carry)
            lax.fori_loop(0, N_EXPERTS, per_expert, 0)

            def per_block(b, carry):
                action(unused_block(b))
                return carry
            lax.fori_loop(n_used_ref[0], n_blocks, per_block, 0)

        for_each(lambda copy: copy.start())
        for_each(lambda copy: copy.wait())

    def start(t, carry):
        for k in range(TOP_K):
            pltpu.make_async_copy(
                x_ref.at[pl.ds(t, 1), :],
                buf_ref.at[pl.ds(dest_ref[base + t * TOP_K + k], 1), :], sem).start()
        return carry

    lax.fori_loop(0, tm, start, 0, unroll=4)
    all_rows = buf_ref.at[pl.ds(0, tm * TOP_K), :]
    pltpu.make_async_copy(all_rows, all_rows, sem).wait()


def _dispatch(x1, dest_flat, pad_start, pad_len, n_used, n_rows):
    T, D = x1.shape
    tm = min(DISPATCH_ROWS, T)
    return pl.pallas_call(
        _dispatch_kernel,
        name="moe_dispatch",
        grid_spec=pltpu.PrefetchScalarGridSpec(
            num_scalar_prefetch=4,
            grid=(T // tm,),
            in_specs=[pl.BlockSpec((tm, D), lambda i, *_: (i, 0))],
            out_specs=pl.BlockSpec(memory_space=pl.ANY),
            scratch_shapes=[pltpu.VMEM((EXPERT_ROWS, D), f32),
                            pltpu.SemaphoreType.DMA(()), pltpu.SemaphoreType.DMA(())]),
        out_shape=jax.ShapeDtypeStruct((n_rows, D), f32),
        compiler_params=_params(1),
    )(dest_flat, pad_start, pad_len, n_used, x1)


def _expert_matmul_kernel(activation, blk_expert_ref, n_used_ref, x_ref, w_ref, b_ref, o_ref, wq_ref):
    i = pl.program_id(0)
    used = i < n_used_ref[0]
    first_of_expert = jnp.logical_or(
        i == 0, blk_expert_ref[i] != blk_expert_ref[jnp.maximum(i - 1, 0)])

    @pl.when(jnp.logical_and(used, first_of_expert))
    def _():
        chunk = math.gcd(WEIGHT_CAST_ROWS, w_ref.shape[0])

        def cast_rows(c, carry):
            rows = pl.ds(pl.multiple_of(c * chunk, chunk), chunk)
            wq_ref[rows, :] = w_ref[rows, :].astype(bf16)
            return carry
        lax.fori_loop(0, w_ref.shape[0] // chunk, cast_rows, 0)

    @pl.when(used)
    def _():
        h = _dot(x_ref[...].astype(bf16), wq_ref[...]) + b_ref[...]
        o_ref[...] = activation(h).astype(o_ref.dtype)

    @pl.when(jnp.logical_not(used))
    def _():
        o_ref[...] = jnp.zeros_like(o_ref)


def _clamped_swiglu(h):
    d_ff = h.shape[1] // 2
    gate = jnp.minimum(h[:, :d_ff], SWIGLU_LIMIT)
    up = jnp.clip(h[:, d_ff:], -SWIGLU_LIMIT, SWIGLU_LIMIT)
    return (up + 1.0) * (gate * jax.nn.sigmoid(gate * SWIGLU_ALPHA))


def _expert_matmul(x_buf, blk_expert, n_used, w, b, layer, activation, out_dtype, name):
    R, K = x_buf.shape
    L, E, _, N = w.shape
    n_out = jax.eval_shape(activation, jax.ShapeDtypeStruct((EXPERT_ROWS, N), f32)).shape[1]
    tm = EXPERT_ROWS
    used = lambda i, be, nu: jnp.minimum(i, nu[0] - 1)
    expert = lambda i, be, nu: (layer, be[i], 0, 0)
    return pl.pallas_call(
        functools.partial(_expert_matmul_kernel, activation),
        name=name,
        grid_spec=pltpu.PrefetchScalarGridSpec(
            num_scalar_prefetch=2,
            grid=(R // tm,),
            in_specs=[pl.BlockSpec((tm, K), lambda i, be, nu: (used(i, be, nu), 0)),
                      pl.BlockSpec((None, None, K, N), expert),
                      pl.BlockSpec((None, None, 1, N), expert)],
            out_specs=pl.BlockSpec((tm, n_out), lambda i, be, nu: (i, 0)),
            scratch_shapes=[pltpu.VMEM((K, N), bf16)]),
        out_shape=jax.ShapeDtypeStruct((R, n_out), out_dtype),
        compiler_params=_params(1),
    )(blk_expert, n_used, x_buf, w, b.reshape(L, E, 1, N))


def _expert_ffn(x_buf, blk_expert, n_used, wgu, bgu, wd, bd, layer):
    act = _expert_matmul(x_buf, blk_expert, n_used, wgu, bgu, layer, _clamped_swiglu, bf16,
                         "expert_gate_up")
    return _expert_matmul(act, blk_expert, n_used, wd, bd, layer, lambda h: h, f32, "expert_down")


def _combine_kernel(alpha, dest_ref, y_hbm, x_ref, gate_ref, lg_ref, lb_ref, o_ref, rows_buf, sems):
    tm = x_ref.shape[0]
    i = pl.program_id(0)
    n = pl.num_programs(0)

    def start_tile(tile, slot):
        def body(t, carry):
            for k in range(TOP_K):
                pltpu.make_async_copy(
                    y_hbm.at[pl.ds(dest_ref[(tile * tm + t) * TOP_K + k], 1), :],
                    rows_buf.at[slot, k, pl.ds(t, 1), :], sems.at[slot]).start()
            return carry
        lax.fori_loop(0, tm, body, 0, unroll=4)

    slot = i % 2

    @pl.when(i == 0)
    def _():
        start_tile(0, 0)

    @pl.when(i + 1 < n)
    def _():
        start_tile(i + 1, 1 - slot)

    for k in range(TOP_K):
        pltpu.make_async_copy(y_hbm.at[pl.ds(0, tm), :], rows_buf.at[slot, k], sems.at[slot]).wait()

    y = jnp.zeros(x_ref.shape, f32)
    for k in range(TOP_K):
        y = y + gate_ref[:, k:k + 1] * rows_buf[slot, k]
    o_ref[...] = _layer_norm(alpha * x_ref[...] + y, lg_ref[...], lb_ref[...])


def _combine(y_buf, dest_flat, x1, gates, lg, lb, alpha):
    T, D = x1.shape
    tm = min(COMBINE_ROWS, T)
    return pl.pallas_call(
        functools.partial(_combine_kernel, alpha),
        name="moe_combine",
        grid_spec=pltpu.PrefetchScalarGridSpec(
            num_scalar_prefetch=1,
            grid=(T // tm,),
            in_specs=[pl.BlockSpec(memory_space=pl.ANY),
                      pl.BlockSpec((tm, D), lambda i, d: (i, 0)),
                      pl.BlockSpec((tm, LANES), lambda i, d: (i, 0)),
                      pl.BlockSpec((1, D), lambda i, d: (0, 0)),
                      pl.BlockSpec((1, D), lambda i, d: (0, 0))],
            out_specs=pl.BlockSpec((tm, D), lambda i, d: (i, 0)),
            scratch_shapes=[pltpu.VMEM((2, TOP_K, tm, D), f32),
                            pltpu.SemaphoreType.DMA((2,))]),
        out_shape=jax.ShapeDtypeStruct((T, D), f32),
        compiler_params=_params(1),
    )(dest_flat, y_buf, x1, gates, lg, lb)


def _moe(x1, idx, gates, pos, counts, wgu, bgu, wd, bd, layer, lg, lb, alpha):
    T, D = x1.shape
    n_assign = T * TOP_K
    counts = counts.reshape(N_EXPERTS)
    padded = (counts + EXPERT_ROWS - 1) // EXPERT_ROWS * EXPERT_ROWS
    group_end = jnp.cumsum(padded)
    group_start = group_end - padded
    experts = jnp.arange(N_EXPERTS, dtype=jnp.int32)
    start_of = jnp.sum(jnp.where(idx[:, :TOP_K, None] == experts, group_start, 0), axis=-1)
    dest = (start_of + pos[:, :TOP_K]).reshape(n_assign).astype(jnp.int32)
    n_blocks = -(-n_assign // EXPERT_ROWS) + N_EXPERTS
    block_row = jnp.arange(n_blocks, dtype=jnp.int32)[:, None] * EXPERT_ROWS
    blk_expert = jnp.minimum(jnp.sum(group_end[None, :] <= block_row, axis=-1),
                             N_EXPERTS - 1).astype(jnp.int32)
    n_used = (group_end[-1:] // EXPERT_ROWS).astype(jnp.int32)
    x_buf = _dispatch(x1, dest, (group_start + counts).astype(jnp.int32),
                      (padded - counts).astype(jnp.int32), n_used, n_blocks * EXPERT_ROWS)
    y_buf = _expert_ffn(x_buf, blk_expert, n_used, wgu, bgu, wd, bd, layer)
    return _combine(y_buf, dest, x1, gates, lg, lb, alpha)


def kernel(x, w_in, g_mix_a, g_mix_b, w_out, ln1_g, ln1_b, w_router, b_router,
           w_gate_up, b_gate_up, w_down, b_down, ln2_g, ln2_b):
    B, S, D = x.shape
    depth = w_in.shape[0]
    wa = g_mix_a.shape[1]
    wb = g_mix_b.shape[1]
    heads_a = wa // HEAD_DIM
    heads_b = wb // HEAD_DIM
    alpha = (2.0 * depth) ** 0.25
    slopes = jnp.asarray([2.0 ** (-8.0 * (h + 1) / heads_a) for h in range(heads_a)], f32)
    row = lambda a: a.reshape(1, -1)

    assert wa == wb, "the q/k/v projection is split into two equal column halves"
    w_in_b = w_in.astype(bf16)
    w_out_b = w_out.astype(bf16)

    xt = x.reshape(B * S, D)
    for l in range(depth):
        ha = _proj(xt, w_in_b, l, 0, f32).reshape(B, S, 3 * wa)
        hb = _proj(xt, w_in_b, l, 1, bf16).reshape(B, S, 3 * wb)
        oa = _dilated_attention(ha, slopes, heads_a).reshape(B * S, wa)
        ob = _stick_attention(hb, heads_b).reshape(B * S, wb)
        x1, idx, gates, pos, counts = _mix_router(
            oa, ob, xt, w_out_b, l, row(g_mix_a[l]), row(g_mix_b[l]),
            row(ln1_g[l]), row(ln1_b[l]), w_router[l], row(b_router[l]), alpha)
        xt = _moe(x1, idx, gates, pos, counts, w_gate_up, b_gate_up, w_down, b_down, l,
                  row(ln2_g[l]), row(ln2_b[l]), alpha)
    return xt.reshape(B, S, D)
```

```python
import functools
import math

import jax
import jax.numpy as jnp
from jax import lax
from jax.experimental import pallas as pl
from jax.experimental.pallas import tpu as pltpu

HEAD_DIM = 64
HEADS_PER_BLOCK = 2
LANES = HEADS_PER_BLOCK * HEAD_DIM
Q_BLOCK = 128
DILATED_PATTERNS = ((128, 1), (512, 4), (2048, 16))
N_EXPERTS = 32
TOP_K = 4
SWIGLU_LIMIT = 7.0
SWIGLU_ALPHA = 1.702
LN_EPS = 1e-5
RMS_EPS = 1e-6
MASKED = -1e30

VMEM_LIMIT = 56 * 1024 * 1024
PROJ_ROWS = 256
ROUTER_ROWS = 512
DISPATCH_ROWS = 128
EXPERT_ROWS = 256
WEIGHT_CAST_ROWS = 256
COMBINE_ROWS = 128
STICK_TILE = 256
LOG2_E = 1.4426950408889634
QUERY_SCALE = LOG2_E / math.sqrt(HEAD_DIM)
STICK_UNDERFLOW_LOG2 = -176.0

f32 = jnp.float32
bf16 = jnp.bfloat16


def _params(n_grid_dims):
    return pltpu.CompilerParams(
        dimension_semantics=("arbitrary",) * n_grid_dims, vmem_limit_bytes=VMEM_LIMIT)


def _dot_nt(a, b):
    return lax.dot_general(a, b, (((1,), (1,)), ((), ())), preferred_element_type=f32)


def _dot(a, b):
    return jnp.dot(a, b, preferred_element_type=f32)


def _proj_kernel(n_query_cols, x_ref, w_ref, o_ref):
    h = _dot(x_ref[...].astype(bf16), w_ref[...])
    query = lax.broadcasted_iota(jnp.int32, (1, h.shape[1]), 1) < n_query_cols
    o_ref[...] = (h * jnp.where(query, QUERY_SCALE, 1.0)).astype(o_ref.dtype)


def _proj(x, w, layer, half, out_dtype):
    T, K = x.shape
    N = w.shape[2] // 2
    tm = min(PROJ_ROWS, T)
    return pl.pallas_call(
        functools.partial(_proj_kernel, N // 3),
        name="proj",
        grid=(T // tm,),
        in_specs=[pl.BlockSpec((tm, K), lambda i: (i, 0)),
                  pl.BlockSpec((None, K, N), lambda i: (layer, 0, half))],
        out_specs=pl.BlockSpec((tm, N), lambda i: (i, 0)),
        out_shape=jax.ShapeDtypeStruct((T, N), out_dtype),
        compiler_params=_params(1),
    )(x, w)


def _dilated_kernel(slopes_ref, q_ref, k_ref, v_ref, o_ref, acc_o, acc_m, acc_l, bias_ref,
                    s0_ref, s1_ref):
    S = q_ref.shape[0]
    QB = Q_BLOCK
    hp = pl.program_id(0)
    head0 = lax.broadcasted_iota(jnp.int32, (QB, LANES), 1) < HEAD_DIM

    def stack_heads(x):
        first = lax.broadcasted_iota(jnp.int32, x.shape, 1) < HEAD_DIM
        return jnp.concatenate([jnp.where(first, x, jnp.zeros_like(x)),
                                jnp.where(first, jnp.zeros_like(x), x)], axis=0)

    def keys_per_step(d):
        return 2 * QB if (S // d) // QB > 1 else QB

    @pl.when(pl.program_id(1) == 0)
    def _():
        for r, (window, d) in enumerate(DILATED_PATTERNS):
            nk = keys_per_step(d)
            srow = lax.broadcasted_iota(jnp.int32, (2 * QB, nk), 0)
            scol = lax.broadcasted_iota(jnp.int32, (2 * QB, nk), 1)
            second = srow >= QB
            coef = jnp.where(second, -(slopes_ref[hp * HEADS_PER_BLOCK + 1] * float(d)),
                             -(slopes_ref[hp * HEADS_PER_BLOCK] * float(d)))
            for variant, offset in enumerate((nk - QB, 0)):
                delta = jnp.where(second, srow - QB, srow) - scol + offset
                ok = jnp.logical_and(delta >= 0, delta <= QB)
                bias_ref[r, variant, :, :nk] = jnp.where(
                    ok, (coef * delta.astype(f32)) * LOG2_E, MASKED)

    for r, (window, d) in enumerate(DILATED_PATTERNS):
        assert window // d == QB
        nblk = (S // d) // QB
        nk = keys_per_step(d)
        n_steps = d * nblk

        def block_rows(idx, d=d, nblk=nblk, nk=nk):
            c = idx // nblk
            nb = idx % nblk
            strided = lambda start, n: pl.ds(start, n, stride=d) if d > 1 else pl.ds(start, n)
            rows = strided(nb * (QB * d) + c, QB)
            krows = strided(jnp.maximum(nb - (nk // QB - 1), 0) * (QB * d) + c, nk)
            variant = jnp.where(nb > 0, 0, 1) if nblk > 1 else 0
            return rows, krows, variant

        def scores(idx, s_ref, r=r, nk=nk):
            rows, krows, variant = block_rows(idx)
            qs = stack_heads(q_ref[rows, :].astype(bf16))
            k = k_ref[krows, :].astype(bf16)
            for h in range(HEADS_PER_BLOCK):
                hrows = slice(h * QB, (h + 1) * QB)
                s_ref[hrows, :nk] = _dot_nt(qs[hrows], k) + bias_ref[r, variant, hrows, :nk]

        def softmax_pv(idx, s_ref, r=r, nk=nk):
            rows, krows, _ = block_rows(idx)
            s = s_ref[:, :nk]
            m = jnp.max(s, axis=-1, keepdims=True)
            p = jnp.exp2(s - m)
            l = jnp.sum(p, axis=-1, keepdims=True)
            p = p.astype(bf16)
            vs = stack_heads(v_ref[krows, :].astype(bf16))
            acc_o[r, rows, :] = _dot(p[:QB], vs[:nk]) + _dot(p[QB:], vs[nk:])
            acc_m[r, rows, :] = jnp.where(head0, m[:QB], m[QB:])
            acc_l[r, rows, :] = jnp.where(head0, l[:QB], l[QB:])

        assert n_steps % 2 == 0
        scores(0, s0_ref)

        def step_pair(j, carry, n_steps=n_steps):
            scores(2 * j + 1, s1_ref)
            softmax_pv(2 * j, s0_ref)
            scores(jnp.minimum(2 * j + 2, n_steps - 1), s0_ref)
            softmax_pv(2 * j + 1, s1_ref)
            return carry

        lax.fori_loop(0, n_steps // 2, step_pair, 0, unroll=4)

    def merge(i, carry):
        rows = pl.ds(pl.multiple_of(i * Q_BLOCK, Q_BLOCK), Q_BLOCK)
        ms = [acc_m[r, rows, :] for r in range(len(DILATED_PATTERNS))]
        m_all = functools.reduce(jnp.maximum, ms)
        num = jnp.zeros((Q_BLOCK, LANES), f32)
        den = jnp.zeros((Q_BLOCK, LANES), f32)
        for r in range(len(DILATED_PATTERNS)):
            w = jnp.exp2(ms[r] - m_all)
            num = num + w * acc_o[r, rows, :]
            den = den + w * acc_l[r, rows, :]
        o_ref[rows, :] = num / den
        return carry

    lax.fori_loop(0, S // Q_BLOCK, merge, 0)


def _dilated_attention(ha, slopes, n_heads):
    B, S, _ = ha.shape
    nhp = n_heads // HEADS_PER_BLOCK
    blk = lambda off: pl.BlockSpec((None, S, LANES), lambda h, b, sl, off=off: (b, 0, off + h))
    return pl.pallas_call(
        _dilated_kernel,
        name="dilated_attn",
        grid_spec=pltpu.PrefetchScalarGridSpec(
            num_scalar_prefetch=1,
            grid=(nhp, B),
            in_specs=[blk(0), blk(nhp), blk(2 * nhp)],
            out_specs=pl.BlockSpec((None, S, LANES), lambda h, b, sl: (b, 0, h)),
            scratch_shapes=[pltpu.VMEM((len(DILATED_PATTERNS), S, LANES), f32)] * 3
            + [pltpu.VMEM((len(DILATED_PATTERNS), 2, 2 * Q_BLOCK, 2 * Q_BLOCK), f32)]
            + [pltpu.VMEM((2 * Q_BLOCK, 2 * Q_BLOCK), f32)] * 2),
        out_shape=jax.ShapeDtypeStruct((B, S, n_heads * HEAD_DIM), f32),
        compiler_params=_params(2),
    )(slopes, ha, ha, ha)


def _stick_kernel(q_ref, k_ref, v_ref, o_ref, later_ref):
    S = q_ref.shape[0]
    T = min(STICK_TILE, S)
    lane = lax.broadcasted_iota(jnp.int32, (T, LANES), 1)
    head0 = lane < HEAD_DIM
    krow = lax.broadcasted_iota(jnp.int32, (T, T), 0)
    kcol = lax.broadcasted_iota(jnp.int32, (T, T), 1)
    later_ref[...] = (krow > kcol).astype(bf16)
    causal = kcol < krow

    def stack_heads(x):
        return jnp.concatenate([jnp.where(head0, x, jnp.zeros_like(x)),
                                jnp.where(head0, jnp.zeros_like(x), x)], axis=0)

    def tile(qs, kb, nearer, acc, diagonal):
        krows = pl.ds(pl.multiple_of(kb * T, T), T)
        k = k_ref[krows, :]
        vs = stack_heads(v_ref[krows, :])
        zs = [_dot_nt(qs[h * T:(h + 1) * T], k) for h in range(HEADS_PER_BLOCK)]
        log_bs, withins, new_nearer = [], [], []
        for z in zs:
            log_b = jnp.minimum(z, 0.0) - jnp.log2(1.0 + jnp.exp2(-jnp.abs(z)))
            log_1mb = log_b - z
            if diagonal:
                log_1mb = jnp.where(causal, log_1mb, 0.0)
            log_bs.append(log_b)
            withins.append(_dot(log_1mb.astype(bf16), later_ref[...]))
            new_nearer.append(nearer[len(log_bs) * T - T:len(log_bs) * T]
                              + jnp.sum(log_1mb, axis=-1, keepdims=True))
        new_nearer = jnp.concatenate(new_nearer, axis=0)
        live = jnp.max(new_nearer) > STICK_UNDERFLOW_LOG2
        for h in range(HEADS_PER_BLOCK):
            a = jnp.exp2(log_bs[h] + withins[h] + nearer[h * T:(h + 1) * T])
            if diagonal:
                a = jnp.where(causal, a, 0.0)
            acc = acc + _dot(a.astype(bf16), vs[h * T:(h + 1) * T])
        return live, new_nearer, acc

    def q_tiles(ns, has_previous):
        qrows = [pl.ds(pl.multiple_of(n * T, T), T) for n in ns]
        qss = [stack_heads(q_ref[rows, :]) for rows in qrows]
        states = [tile(qs, n, jnp.zeros((2 * T, 1), f32), jnp.zeros((T, LANES), f32), True)
                  for qs, n in zip(qss, ns)]
        walked = 0
        if has_previous:
            states = [tile(qs, n - 1, st[1], st[2], False) for qs, n, st in zip(qss, ns, states)]
            walked = 1

        for rows, qs, n, state in zip(qrows, qss, ns, states):
            def more(state, n=n):
                i, live, _, _ = state
                return jnp.logical_and(i < n, live)

            def farther(state, n=n, qs=qs):
                i, _, nearer, acc = state
                return (i + 1,) + tile(qs, n - 1 - i, nearer, acc, False)

            _, _, _, acc = lax.while_loop(more, farther, (jnp.int32(walked),) + state)
            o_ref[rows, :] = acc

    n_q = S // T
    q_tiles([0], False)

    def later_pair(j, carry):
        q_tiles([1 + 2 * j, 2 + 2 * j], True)
        return carry

    lax.fori_loop(0, (n_q - 1) // 2, later_pair, 0)
    if (n_q - 1) % 2:
        q_tiles([n_q - 1], True)


def _stick_attention(hb, n_heads):
    B, S, _ = hb.shape
    nhp = n_heads // HEADS_PER_BLOCK
    T = min(STICK_TILE, S)
    blk = lambda off: pl.BlockSpec((None, S, LANES), lambda b, h, off=off: (b, 0, off + h))
    return pl.pallas_call(
        _stick_kernel,
        name="stick_attn",
        grid=(B, nhp),
        in_specs=[blk(0), blk(nhp), blk(2 * nhp)],
        out_specs=pl.BlockSpec((None, S, LANES), lambda b, h: (b, 0, h)),
        out_shape=jax.ShapeDtypeStruct((B, S, n_heads * HEAD_DIM), f32),
        scratch_shapes=[pltpu.VMEM((T, T), bf16)],
        compiler_params=_params(2),
    )(hb, hb, hb)


def _layer_norm(u, g, b):
    mu = jnp.mean(u, axis=-1, keepdims=True)
    var = jnp.mean(jnp.square(u - mu), axis=-1, keepdims=True)
    return (u - mu) * lax.rsqrt(var + LN_EPS) * g + b


def _rms_norm(o, g):
    return o * lax.rsqrt(jnp.mean(jnp.square(o), axis=-1, keepdims=True) + RMS_EPS) * g


def _mix_router_kernel(alpha, oa_ref, ob_ref, x_ref, wo_ref, ga_ref, gb_ref, lg_ref, lb_ref,
                       wr_ref, br_ref, x1_ref, x1s_ref, idx_ref, gate_ref, pos_ref, cnt_ref, running,
                       wr_hi, wr_lo):
    tm = x_ref.shape[0]
    wa = oa_ref.shape[1]

    @pl.when(pl.program_id(0) == 0)
    def _():
        running[...] = jnp.zeros_like(running)
        w_hi = wr_ref[...].astype(bf16)
        wr_hi[...] = w_hi
        wr_lo[...] = (wr_ref[...] - w_hi.astype(f32)).astype(bf16)

    mix_a = _rms_norm(oa_ref[...], ga_ref[...]).astype(bf16)
    mix_b = _rms_norm(ob_ref[...], gb_ref[...]).astype(bf16)
    y = _dot(mix_a, wo_ref[:wa, :]) + _dot(mix_b, wo_ref[wa:, :])
    x1 = _layer_norm(alpha * x_ref[...] + y, lg_ref[...], lb_ref[...])
    x1_ref[...] = x1
    n_chunks = x1.shape[1] // LANES
    for c in range(n_chunks):
        x1s_ref[pl.ds(c, tm, stride=n_chunks), :] = x1[:, c * LANES:(c + 1) * LANES]

    x_hi = x1.astype(bf16)
    x_lo = (x1 - x_hi.astype(f32)).astype(bf16)
    logits = (_dot(x_hi, wr_hi[...]) + (_dot(x_hi, wr_lo[...]) + _dot(x_lo, wr_hi[...]))) + br_ref[...]
    lane = lax.broadcasted_iota(jnp.int32, (tm, N_EXPERTS), 1)
    work = logits
    vals, sels, idxs = [], [], []
    for _ in range(TOP_K):
        m = jnp.max(work, axis=-1, keepdims=True)
        idx = jnp.min(jnp.where(work == m, lane, N_EXPERTS), axis=-1, keepdims=True)
        sel = lane == idx
        work = jnp.where(sel, -jnp.inf, work)
        vals.append(m)
        sels.append(sel)
        idxs.append(idx)
    exps = [jnp.exp(v - vals[0]) for v in vals]
    denom = functools.reduce(jnp.add, exps)

    chosen = functools.reduce(jnp.logical_or, sels)
    trow = lax.broadcasted_iota(jnp.int32, (tm, tm), 0)
    tcol = lax.broadcasted_iota(jnp.int32, (tm, tm), 1)
    earlier = (tcol < trow).astype(bf16)
    rank = _dot(earlier, chosen.astype(bf16)) + running[...]
    running[...] = running[...] + jnp.sum(chosen.astype(f32), axis=0, keepdims=True)
    cnt_ref[...] = running[...].astype(jnp.int32)

    out_lane = lax.broadcasted_iota(jnp.int32, (tm, LANES), 1)
    idx_out = jnp.zeros((tm, LANES), jnp.int32)
    gate_out = jnp.zeros((tm, LANES), f32)
    pos_out = jnp.zeros((tm, LANES), jnp.int32)
    for k in range(TOP_K):
        pos = jnp.sum(jnp.where(sels[k], rank, 0.0), axis=-1, keepdims=True).astype(jnp.int32)
        idx_out = jnp.where(out_lane == k, idxs[k], idx_out)
        gate_out = jnp.where(out_lane == k, exps[k] / denom, gate_out)
        pos_out = jnp.where(out_lane == k, pos, pos_out)
    idx_ref[...] = idx_out
    gate_ref[...] = gate_out
    pos_ref[...] = pos_out


def _mix_router(oa, ob, x, wo, layer, ga, gb, lg, lb, wr, br, alpha):
    T, D = x.shape
    W = oa.shape[1]
    tm = min(ROUTER_ROWS, T)
    rows = lambda n: pl.BlockSpec((tm, n), lambda i: (i, 0))
    full = lambda a: pl.BlockSpec(a.shape, lambda i: (0,) * a.ndim)
    return pl.pallas_call(
        functools.partial(_mix_router_kernel, alpha),
        name="mix_router",
        grid=(T // tm,),
        in_specs=[rows(W), rows(W), rows(D), pl.BlockSpec((None, D, D), lambda i: (layer, 0, 0)),
                  full(ga), full(gb), full(lg), full(lb), full(wr), full(br)],
        out_specs=[rows(D), pl.BlockSpec((tm * (D // LANES), LANES), lambda i: (i, 0)),
                   rows(LANES), rows(LANES), rows(LANES),
                   pl.BlockSpec((1, N_EXPERTS), lambda i: (0, 0))],
        out_shape=[jax.ShapeDtypeStruct((T, D), f32),
                   jax.ShapeDtypeStruct((T * (D // LANES), LANES), f32),
                   jax.ShapeDtypeStruct((T, LANES), jnp.int32),
                   jax.ShapeDtypeStruct((T, LANES), f32),
                   jax.ShapeDtypeStruct((T, LANES), jnp.int32),
                   jax.ShapeDtypeStruct((1, N_EXPERTS), jnp.int32)],
        scratch_shapes=[pltpu.VMEM((1, N_EXPERTS), f32),
                        pltpu.VMEM(wr.shape, bf16), pltpu.VMEM(wr.shape, bf16)],
        compiler_params=_params(1),
    )(oa, ob, x, wo, ga, gb, lg, lb, wr, br)


def _dispatch_kernel(dest_ref, pad_start_ref, pad_len_ref, n_used_ref, x_ref, buf_ref,
                     zeros_ref, sem, zero_sem):
    slab = zeros_ref.shape[0] // EXPERT_ROWS
    tm = x_ref.shape[0] // slab
    i = pl.program_id(0)
    base = i * (tm * TOP_K)
    n_blocks = buf_ref.shape[0] // zeros_ref.shape[0]

    def token(t):
        return pl.ds(pl.multiple_of(t * slab, slab), slab)

    @pl.when(i == 0)
    def _():
        zeros_ref[...] = jnp.zeros_like(zeros_ref)

        def pad_row(e, j):
            return pltpu.make_async_copy(
                zeros_ref.at[token(0), :], buf_ref.at[token(pad_start_ref[e] + j), :], zero_sem)

        def unused_block(b):
            rows = pl.ds(pl.multiple_of(b * zeros_ref.shape[0], zeros_ref.shape[0]), zeros_ref.shape[0])
            return pltpu.make_async_copy(zeros_ref, buf_ref.at[rows, :], zero_sem)

        def for_each(action):
            def per_expert(e, carry):
                def per_row(j, c):
                    action(pad_row(e, j))
                    return c
                return lax.fori_loop(0, pad_len_ref[e], per_row, carry)
            lax.fori_loop(0, N_EXPERTS, per_expert, 0)

            def per_block(b, carry):
                action(unused_block(b))
                return carry
            lax.fori_loop(n_used_ref[0], n_blocks, per_block, 0)

        for_each(lambda copy: copy.start())
        for_each(lambda copy: copy.wait())

    def start(t, carry):
        for k in range(TOP_K):
            pltpu.make_async_copy(
                x_ref.at[token(t), :],
                buf_ref.at[token(dest_ref[base + t * TOP_K + k]), :], sem).start()
        return carry

    lax.fori_loop(0, tm, start, 0, unroll=4)
    all_rows = buf_ref.at[pl.ds(0, tm * TOP_K * slab), :]
    pltpu.make_async_copy(all_rows, all_rows, sem).wait()


def _dispatch(x1_slabs, dest_flat, pad_start, pad_len, n_used, n_rows, slab):
    T = x1_slabs.shape[0] // slab
    tm = min(DISPATCH_ROWS, T)
    return pl.pallas_call(
        _dispatch_kernel,
        name="moe_dispatch",
        grid_spec=pltpu.PrefetchScalarGridSpec(
            num_scalar_prefetch=4,
            grid=(T // tm,),
            in_specs=[pl.BlockSpec((tm * slab, LANES), lambda i, *_: (i, 0))],
            out_specs=pl.BlockSpec(memory_space=pl.ANY),
            scratch_shapes=[pltpu.VMEM((EXPERT_ROWS * slab, LANES), f32),
                            pltpu.SemaphoreType.DMA(()), pltpu.SemaphoreType.DMA(())]),
        out_shape=jax.ShapeDtypeStruct((n_rows * slab, LANES), f32),
        compiler_params=_params(1),
    )(dest_flat, pad_start, pad_len, n_used, x1_slabs)


def _expert_matmul_kernel(activation, slab, blk_expert_ref, n_used_ref, x_ref, w_ref, b_ref, o_ref,
                          wq_ref):
    i = pl.program_id(0)
    used = i < n_used_ref[0]
    first_of_expert = jnp.logical_or(
        i == 0, blk_expert_ref[i] != blk_expert_ref[jnp.maximum(i - 1, 0)])

    @pl.when(jnp.logical_and(used, first_of_expert))
    def _():
        chunk = math.gcd(WEIGHT_CAST_ROWS, w_ref.shape[0])

        def cast_rows(c, carry):
            rows = pl.ds(pl.multiple_of(c * chunk, chunk), chunk)
            wq_ref[rows, :] = w_ref[rows, :].astype(bf16)
            return carry
        lax.fori_loop(0, w_ref.shape[0] // chunk, cast_rows, 0)

    @pl.when(used)
    def _():
        if slab:
            x = jnp.concatenate([x_ref[pl.ds(c, EXPERT_ROWS, stride=slab), :].astype(bf16)
                                 for c in range(slab)], axis=1)
        else:
            x = x_ref[...].astype(bf16)
        h = _dot(x, wq_ref[...]) + b_ref[...]
        o_ref[...] = activation(h).astype(o_ref.dtype)

    @pl.when(jnp.logical_not(used))
    def _():
        o_ref[...] = jnp.zeros_like(o_ref)


def _clamped_swiglu(h):
    d_ff = h.shape[1] // 2
    gate = jnp.minimum(h[:, :d_ff], SWIGLU_LIMIT)
    up = jnp.clip(h[:, d_ff:], -SWIGLU_LIMIT, SWIGLU_LIMIT)
    return (up + 1.0) * (gate * jax.nn.sigmoid(gate * SWIGLU_ALPHA))


def _expert_matmul(x_buf, blk_expert, n_used, w, b, layer, activation, out_dtype, name, slab=0):
    L, E, K, N = w.shape
    R = x_buf.shape[0] // max(slab, 1)
    x_block = (EXPERT_ROWS * slab, LANES) if slab else (EXPERT_ROWS, K)
    n_out = jax.eval_shape(activation, jax.ShapeDtypeStruct((EXPERT_ROWS, N), f32)).shape[1]
    tm = EXPERT_ROWS
    used = lambda i, be, nu: jnp.minimum(i, nu[0] - 1)
    expert = lambda i, be, nu: (layer, be[i], 0, 0)
    return pl.pallas_call(
        functools.partial(_expert_matmul_kernel, activation, slab),
        name=name,
        grid_spec=pltpu.PrefetchScalarGridSpec(
            num_scalar_prefetch=2,
            grid=(R // tm,),
            in_specs=[pl.BlockSpec(x_block, lambda i, be, nu: (used(i, be, nu), 0)),
                      pl.BlockSpec((None, None, K, N), expert),
                      pl.BlockSpec((None, None, 1, N), expert)],
            out_specs=pl.BlockSpec((tm, n_out), lambda i, be, nu: (i, 0)),
            scratch_shapes=[pltpu.VMEM((K, N), bf16)]),
        out_shape=jax.ShapeDtypeStruct((R, n_out), out_dtype),
        compiler_params=_params(1),
    )(blk_expert, n_used, x_buf, w, b.reshape(L, E, 1, N))


def _expert_ffn(x_buf, blk_expert, n_used, wgu, bgu, wd, bd, layer):
    act = _expert_matmul(x_buf, blk_expert, n_used, wgu, bgu, layer, _clamped_swiglu, bf16,
                         "expert_gate_up", slab=wgu.shape[2] // LANES)
    return _expert_matmul(act, blk_expert, n_used, wd, bd, layer, lambda h: h, f32, "expert_down")


def _combine_kernel(alpha, dest_ref, y_hbm, x_ref, gate_ref, lg_ref, lb_ref, o_ref, rows_buf, sems):
    tm = x_ref.shape[0]
    i = pl.program_id(0)
    n = pl.num_programs(0)

    def start_tile(tile, slot):
        def body(t, carry):
            for k in range(TOP_K):
                pltpu.make_async_copy(
                    y_hbm.at[pl.ds(dest_ref[(tile * tm + t) * TOP_K + k], 1), :],
                    rows_buf.at[slot, k, pl.ds(t, 1), :], sems.at[slot]).start()
            return carry
        lax.fori_loop(0, tm, body, 0, unroll=4)

    slot = i % 2

    @pl.when(i == 0)
    def _():
        start_tile(0, 0)

    @pl.when(i + 1 < n)
    def _():
        start_tile(i + 1, 1 - slot)

    for k in range(TOP_K):
        pltpu.make_async_copy(y_hbm.at[pl.ds(0, tm), :], rows_buf.at[slot, k], sems.at[slot]).wait()

    y = jnp.zeros(x_ref.shape, f32)
    for k in range(TOP_K):
        y = y + gate_ref[:, k:k + 1] * rows_buf[slot, k]
    o_ref[...] = _layer_norm(alpha * x_ref[...] + y, lg_ref[...], lb_ref[...])


def _combine(y_buf, dest_flat, x1, gates, lg, lb, alpha):
    T, D = x1.shape
    tm = min(COMBINE_ROWS, T)
    return pl.pallas_call(
        functools.partial(_combine_kernel, alpha),
        name="moe_combine",
        grid_spec=pltpu.PrefetchScalarGridSpec(
            num_scalar_prefetch=1,
            grid=(T // tm,),
            in_specs=[pl.BlockSpec(memory_space=pl.ANY),
                      pl.BlockSpec((tm, D), lambda i, d: (i, 0)),
                      pl.BlockSpec((tm, LANES), lambda i, d: (i, 0)),
                      pl.BlockSpec((1, D), lambda i, d: (0, 0)),
                      pl.BlockSpec((1, D), lambda i, d: (0, 0))],
            out_specs=pl.BlockSpec((tm, D), lambda i, d: (i, 0)),
            scratch_shapes=[pltpu.VMEM((2, TOP_K, tm, D), f32),
                            pltpu.SemaphoreType.DMA((2,))]),
        out_shape=jax.ShapeDtypeStruct((T, D), f32),
        compiler_params=_params(1),
    )(dest_flat, y_buf, x1, gates, lg, lb)


def _moe(x1, x1_slabs, idx, gates, pos, counts, wgu, bgu, wd, bd, layer, lg, lb, alpha):
    T, D = x1.shape
    n_assign = T * TOP_K
    counts = counts.reshape(N_EXPERTS)
    padded = (counts + EXPERT_ROWS - 1) // EXPERT_ROWS * EXPERT_ROWS
    group_end = jnp.cumsum(padded)
    group_start = group_end - padded
    experts = jnp.arange(N_EXPERTS, dtype=jnp.int32)
    start_of = jnp.sum(jnp.where(idx[:, :TOP_K, None] == experts, group_start, 0), axis=-1)
    dest = (start_of + pos[:, :TOP_K]).reshape(n_assign).astype(jnp.int32)
    n_blocks = -(-n_assign // EXPERT_ROWS) + N_EXPERTS
    block_row = jnp.arange(n_blocks, dtype=jnp.int32)[:, None] * EXPERT_ROWS
    blk_expert = jnp.minimum(jnp.sum(group_end[None, :] <= block_row, axis=-1),
                             N_EXPERTS - 1).astype(jnp.int32)
    n_used = (group_end[-1:] // EXPERT_ROWS).astype(jnp.int32)
    x_buf = _dispatch(x1_slabs, dest, (group_start + counts).astype(jnp.int32),
                      (padded - counts).astype(jnp.int32), n_used, n_blocks * EXPERT_ROWS,
                      D // LANES)
    y_buf = _expert_ffn(x_buf, blk_expert, n_used, wgu, bgu, wd, bd, layer)
    return _combine(y_buf, dest, x1, gates, lg, lb, alpha)


def kernel(x, w_in, g_mix_a, g_mix_b, w_out, ln1_g, ln1_b, w_router, b_router,
           w_gate_up, b_gate_up, w_down, b_down, ln2_g, ln2_b):
    B, S, D = x.shape
    depth = w_in.shape[0]
    wa = g_mix_a.shape[1]
    wb = g_mix_b.shape[1]
    heads_a = wa // HEAD_DIM
    heads_b = wb // HEAD_DIM
    alpha = (2.0 * depth) ** 0.25
    slopes = jnp.asarray([2.0 ** (-8.0 * (h + 1) / heads_a) for h in range(heads_a)], f32)
    row = lambda a: a.reshape(1, -1)

    assert wa == wb, "the q/k/v projection is split into two equal column halves"
    w_in_b = w_in.astype(bf16)
    w_out_b = w_out.astype(bf16)

    xt = x.reshape(B * S, D)
    for l in range(depth):
        ha = _proj(xt, w_in_b, l, 0, f32).reshape(B, S, 3 * wa)
        hb = _proj(xt, w_in_b, l, 1, bf16).reshape(B, S, 3 * wb)
        oa = _dilated_attention(ha, slopes, heads_a).reshape(B * S, wa)
        ob = _stick_attention(hb, heads_b).reshape(B * S, wb)
        x1, x1_slabs, idx, gates, pos, counts = _mix_router(
            oa, ob, xt, w_out_b, l, row(g_mix_a[l]), row(g_mix_b[l]),
            row(ln1_g[l]), row(ln1_b[l]), w_router[l], row(b_router[l]), alpha)
        xt = _moe(x1, x1_slabs, idx, gates, pos, counts, w_gate_up, b_gate_up, w_down, b_down, l,
                  row(ln2_g[l]), row(ln2_b[l]), alpha)
    return xt.reshape(B, S, D)
```

```python
import functools
import math

import jax
import jax.numpy as jnp
from jax import lax
from jax.experimental import pallas as pl
from jax.experimental.pallas import tpu as pltpu

HEAD_DIM = 64
HEADS_PER_BLOCK = 2
LANES = HEADS_PER_BLOCK * HEAD_DIM
Q_BLOCK = 128
DILATED_PATTERNS = ((128, 1), (512, 4), (2048, 16))
N_EXPERTS = 32
TOP_K = 4
SWIGLU_LIMIT = 7.0
SWIGLU_ALPHA = 1.702
LN_EPS = 1e-5
RMS_EPS = 1e-6
MASKED = -1e30

VMEM_LIMIT = 56 * 1024 * 1024
PROJ_ROWS = 256
ROUTER_ROWS = 512
DISPATCH_ROWS = 128
EXPERT_ROWS = 256
WEIGHT_CAST_ROWS = 256
COMBINE_ROWS = 128
STICK_TILE = 256
LOG2_E = 1.4426950408889634
QUERY_SCALE = LOG2_E / math.sqrt(HEAD_DIM)
STICK_UNDERFLOW_LOG2 = -176.0

f32 = jnp.float32
bf16 = jnp.bfloat16


def _params(n_grid_dims):
    return pltpu.CompilerParams(
        dimension_semantics=("arbitrary",) * n_grid_dims, vmem_limit_bytes=VMEM_LIMIT)


def _dot_nt(a, b):
    return lax.dot_general(a, b, (((1,), (1,)), ((), ())), preferred_element_type=f32)


def _dot(a, b):
    return jnp.dot(a, b, preferred_element_type=f32)


def _proj_kernel(n_query_cols, x_ref, w_ref, o_ref):
    h = _dot(x_ref[...].astype(bf16), w_ref[...])
    query = lax.broadcasted_iota(jnp.int32, (1, h.shape[1]), 1) < n_query_cols
    o_ref[...] = (h * jnp.where(query, QUERY_SCALE, 1.0)).astype(o_ref.dtype)


def _proj(x, w, layer, half, out_dtype):
    T, K = x.shape
    N = w.shape[2] // 2
    tm = min(PROJ_ROWS, T)
    return pl.pallas_call(
        functools.partial(_proj_kernel, N // 3),
        name="proj",
        grid=(T // tm,),
        in_specs=[pl.BlockSpec((tm, K), lambda i: (i, 0)),
                  pl.BlockSpec((None, K, N), lambda i: (layer, 0, half))],
        out_specs=pl.BlockSpec((tm, N), lambda i: (i, 0)),
        out_shape=jax.ShapeDtypeStruct((T, N), out_dtype),
        compiler_params=_params(1),
    )(x, w)


def _dilated_kernel(slopes_ref, q_ref, k_ref, v_ref, o_ref, acc_o, acc_m, acc_l, bias_ref,
                    s0_ref, s1_ref):
    S = q_ref.shape[0]
    QB = Q_BLOCK
    hp = pl.program_id(0)
    head0 = lax.broadcasted_iota(jnp.int32, (QB, LANES), 1) < HEAD_DIM

    def stack_heads(x):
        first = lax.broadcasted_iota(jnp.int32, x.shape, 1) < HEAD_DIM
        return jnp.concatenate([jnp.where(first, x, jnp.zeros_like(x)),
                                jnp.where(first, jnp.zeros_like(x), x)], axis=0)

    def keys_per_step(d):
        return 2 * QB if (S // d) // QB > 1 else QB

    @pl.when(pl.program_id(1) == 0)
    def _():
        for r, (window, d) in enumerate(DILATED_PATTERNS):
            nk = keys_per_step(d)
            srow = lax.broadcasted_iota(jnp.int32, (2 * QB, nk), 0)
            scol = lax.broadcasted_iota(jnp.int32, (2 * QB, nk), 1)
            second = srow >= QB
            coef = jnp.where(second, -(slopes_ref[hp * HEADS_PER_BLOCK + 1] * float(d)),
                             -(slopes_ref[hp * HEADS_PER_BLOCK] * float(d)))
            for variant, offset in enumerate((nk - QB, 0)):
                delta = jnp.where(second, srow - QB, srow) - scol + offset
                ok = jnp.logical_and(delta >= 0, delta <= QB)
                bias_ref[r, variant, :, :nk] = jnp.where(
                    ok, (coef * delta.astype(f32)) * LOG2_E, MASKED)

    for r, (window, d) in enumerate(DILATED_PATTERNS):
        assert window // d == QB
        nblk = (S // d) // QB
        nk = keys_per_step(d)
        n_steps = d * nblk

        def block_rows(idx, d=d, nblk=nblk, nk=nk):
            c = idx // nblk
            nb = idx % nblk
            strided = lambda start, n: pl.ds(start, n, stride=d) if d > 1 else pl.ds(start, n)
            rows = strided(nb * (QB * d) + c, QB)
            krows = strided(jnp.maximum(nb - (nk // QB - 1), 0) * (QB * d) + c, nk)
            variant = jnp.where(nb > 0, 0, 1) if nblk > 1 else 0
            return rows, krows, variant

        def scores(idx, s_ref, r=r, nk=nk):
            rows, krows, variant = block_rows(idx)
            qs = stack_heads(q_ref[rows, :].astype(bf16))
            k = k_ref[krows, :].astype(bf16)
            for h in range(HEADS_PER_BLOCK):
                hrows = slice(h * QB, (h + 1) * QB)
                s_ref[hrows, :nk] = _dot_nt(qs[hrows], k) + bias_ref[r, variant, hrows, :nk]

        def softmax_pv(idx, s_ref, r=r, nk=nk):
            rows, krows, _ = block_rows(idx)
            s = s_ref[:, :nk]
            m = jnp.max(s, axis=-1, keepdims=True)
            p = jnp.exp2(s - m)
            l = jnp.sum(p, axis=-1, keepdims=True)
            p = p.astype(bf16)
            vs = stack_heads(v_ref[krows, :].astype(bf16))
            acc_o[r, rows, :] = _dot(p[:QB], vs[:nk]) + _dot(p[QB:], vs[nk:])
            acc_m[r, rows, :] = jnp.where(head0, m[:QB], m[QB:])
            acc_l[r, rows, :] = jnp.where(head0, l[:QB], l[QB:])

        assert n_steps % 2 == 0
        scores(0, s0_ref)

        def step_pair(j, carry, n_steps=n_steps):
            scores(2 * j + 1, s1_ref)
            softmax_pv(2 * j, s0_ref)
            scores(jnp.minimum(2 * j + 2, n_steps - 1), s0_ref)
            softmax_pv(2 * j + 1, s1_ref)
            return carry

        lax.fori_loop(0, n_steps // 2, step_pair, 0, unroll=4)

    def merge(i, carry):
        rows = pl.ds(pl.multiple_of(i * Q_BLOCK, Q_BLOCK), Q_BLOCK)
        ms = [acc_m[r, rows, :] for r in range(len(DILATED_PATTERNS))]
        m_all = functools.reduce(jnp.maximum, ms)
        num = jnp.zeros((Q_BLOCK, LANES), f32)
        den = jnp.zeros((Q_BLOCK, LANES), f32)
        for r in range(len(DILATED_PATTERNS)):
            w = jnp.exp2(ms[r] - m_all)
            num = num + w * acc_o[r, rows, :]
            den = den + w * acc_l[r, rows, :]
        o_ref[rows, :] = num / den
        return carry

    lax.fori_loop(0, S // Q_BLOCK, merge, 0)


def _dilated_attention(ha, slopes, n_heads):
    B, S, _ = ha.shape
    nhp = n_heads // HEADS_PER_BLOCK
    blk = lambda off: pl.BlockSpec((None, S, LANES), lambda h, b, sl, off=off: (b, 0, off + h))
    return pl.pallas_call(
        _dilated_kernel,
        name="dilated_attn",
        grid_spec=pltpu.PrefetchScalarGridSpec(
            num_scalar_prefetch=1,
            grid=(nhp, B),
            in_specs=[blk(0), blk(nhp), blk(2 * nhp)],
            out_specs=pl.BlockSpec((None, S, LANES), lambda h, b, sl: (b, 0, h)),
            scratch_shapes=[pltpu.VMEM((len(DILATED_PATTERNS), S, LANES), f32)] * 3
            + [pltpu.VMEM((len(DILATED_PATTERNS), 2, 2 * Q_BLOCK, 2 * Q_BLOCK), f32)]
            + [pltpu.VMEM((2 * Q_BLOCK, 2 * Q_BLOCK), f32)] * 2),
        out_shape=jax.ShapeDtypeStruct((B, S, n_heads * HEAD_DIM), f32),
        compiler_params=_params(2),
    )(slopes, ha, ha, ha)


def _stick_kernel(q_ref, k_ref, v_ref, o_ref, later_ref):
    S = q_ref.shape[0]
    T = min(STICK_TILE, S)
    lane = lax.broadcasted_iota(jnp.int32, (T, LANES), 1)
    head0 = lane < HEAD_DIM
    krow = lax.broadcasted_iota(jnp.int32, (T, T), 0)
    kcol = lax.broadcasted_iota(jnp.int32, (T, T), 1)
    later_ref[...] = (krow > kcol).astype(bf16)
    causal = kcol < krow

    def stack_heads(x):
        return jnp.concatenate([jnp.where(head0, x, jnp.zeros_like(x)),
                                jnp.where(head0, jnp.zeros_like(x), x)], axis=0)

    def tile(qs, kb, nearer, acc, diagonal):
        krows = pl.ds(pl.multiple_of(kb * T, T), T)
        k = k_ref[krows, :]
        vs = stack_heads(v_ref[krows, :])
        zs = [_dot_nt(qs[h * T:(h + 1) * T], k) for h in range(HEADS_PER_BLOCK)]
        log_bs, withins, new_nearer = [], [], []
        for z in zs:
            log_b = jnp.minimum(z, 0.0) - jnp.log2(1.0 + jnp.exp2(-jnp.abs(z)))
            log_1mb = log_b - z
            if diagonal:
                log_1mb = jnp.where(causal, log_1mb, 0.0)
            log_bs.append(log_b)
            withins.append(_dot(log_1mb.astype(bf16), later_ref[...]))
            new_nearer.append(nearer[len(log_bs) * T - T:len(log_bs) * T]
                              + jnp.sum(log_1mb, axis=-1, keepdims=True))
        new_nearer = jnp.concatenate(new_nearer, axis=0)
        live = jnp.max(new_nearer) > STICK_UNDERFLOW_LOG2
        for h in range(HEADS_PER_BLOCK):
            a = jnp.exp2(log_bs[h] + withins[h] + nearer[h * T:(h + 1) * T])
            if diagonal:
                a = jnp.where(causal, a, 0.0)
            acc = acc + _dot(a.astype(bf16), vs[h * T:(h + 1) * T])
        return live, new_nearer, acc

    def q_tiles(ns, has_previous):
        qrows = [pl.ds(pl.multiple_of(n * T, T), T) for n in ns]
        qss = [stack_heads(q_ref[rows, :]) for rows in qrows]
        states = [tile(qs, n, jnp.zeros((2 * T, 1), f32), jnp.zeros((T, LANES), f32), True)
                  for qs, n in zip(qss, ns)]
        walked = 0
        if has_previous:
            states = [tile(qs, n - 1, st[1], st[2], False) for qs, n, st in zip(qss, ns, states)]
            walked = 1

        for rows, qs, n, state in zip(qrows, qss, ns, states):
            def more(state, n=n):
                i, live, _, _ = state
                return jnp.logical_and(i < n, live)

            def farther(state, n=n, qs=qs):
                i, _, nearer, acc = state
                return (i + 1,) + tile(qs, n - 1 - i, nearer, acc, False)

            _, _, _, acc = lax.while_loop(more, farther, (jnp.int32(walked),) + state)
            o_ref[rows, :] = acc

    n_q = S // T
    q_tiles([0], False)

    def later_pair(j, carry):
        q_tiles([1 + 2 * j, 2 + 2 * j], True)
        return carry

    lax.fori_loop(0, (n_q - 1) // 2, later_pair, 0)
    if (n_q - 1) % 2:
        q_tiles([n_q - 1], True)


def _stick_attention(hb, n_heads):
    B, S, _ = hb.shape
    nhp = n_heads // HEADS_PER_BLOCK
    T = min(STICK_TILE, S)
    blk = lambda off: pl.BlockSpec((None, S, LANES), lambda b, h, off=off: (b, 0, off + h))
    return pl.pallas_call(
        _stick_kernel,
        name="stick_attn",
        grid=(B, nhp),
        in_specs=[blk(0), blk(nhp), blk(2 * nhp)],
        out_specs=pl.BlockSpec((None, S, LANES), lambda b, h: (b, 0, h)),
        out_shape=jax.ShapeDtypeStruct((B, S, n_heads * HEAD_DIM), f32),
        scratch_shapes=[pltpu.VMEM((T, T), bf16)],
        compiler_params=_params(2),
    )(hb, hb, hb)


def _layer_norm(u, g, b):
    mu = jnp.mean(u, axis=-1, keepdims=True)
    var = jnp.mean(jnp.square(u - mu), axis=-1, keepdims=True)
    return (u - mu) * lax.rsqrt(var + LN_EPS) * g + b


def _rms_norm(o, g):
    return o * lax.rsqrt(jnp.mean(jnp.square(o), axis=-1, keepdims=True) + RMS_EPS) * g


def _mix_router_kernel(alpha, oa_ref, ob_ref, x_ref, wo_ref, ga_ref, gb_ref, lg_ref, lb_ref,
                       wr_ref, br_ref, x1_ref, idx_ref, gate_ref, pos_ref, cnt_ref, running,
                       wr_hi, wr_lo):
    tm = x_ref.shape[0]
    wa = oa_ref.shape[1]

    @pl.when(pl.program_id(0) == 0)
    def _():
        running[...] = jnp.zeros_like(running)
        w_hi = wr_ref[...].astype(bf16)
        wr_hi[...] = w_hi
        wr_lo[...] = (wr_ref[...] - w_hi.astype(f32)).astype(bf16)

    mix_a = _rms_norm(oa_ref[...], ga_ref[...]).astype(bf16)
    mix_b = _rms_norm(ob_ref[...], gb_ref[...]).astype(bf16)
    y = _dot(mix_a, wo_ref[:wa, :]) + _dot(mix_b, wo_ref[wa:, :])
    x1 = _layer_norm(alpha * x_ref[...] + y, lg_ref[...], lb_ref[...])
    x1_ref[...] = x1

    x_hi = x1.astype(bf16)
    x_lo = (x1 - x_hi.astype(f32)).astype(bf16)
    logits = (_dot(x_hi, wr_hi[...]) + (_dot(x_hi, wr_lo[...]) + _dot(x_lo, wr_hi[...]))) + br_ref[...]
    lane = lax.broadcasted_iota(jnp.int32, (tm, N_EXPERTS), 1)
    work = logits
    vals, sels, idxs = [], [], []
    for _ in range(TOP_K):
        m = jnp.max(work, axis=-1, keepdims=True)
        idx = jnp.min(jnp.where(work == m, lane, N_EXPERTS), axis=-1, keepdims=True)
        sel = lane == idx
        work = jnp.where(sel, -jnp.inf, work)
        vals.append(m)
        sels.append(sel)
        idxs.append(idx)
    exps = [jnp.exp(v - vals[0]) for v in vals]
    denom = functools.reduce(jnp.add, exps)

    chosen = functools.reduce(jnp.logical_or, sels)
    trow = lax.broadcasted_iota(jnp.int32, (tm, tm), 0)
    tcol = lax.broadcasted_iota(jnp.int32, (tm, tm), 1)
    earlier = (tcol < trow).astype(bf16)
    rank = _dot(earlier, chosen.astype(bf16)) + running[...]
    running[...] = running[...] + jnp.sum(chosen.astype(f32), axis=0, keepdims=True)
    cnt_ref[...] = running[...].astype(jnp.int32)

    out_lane = lax.broadcasted_iota(jnp.int32, (tm, LANES), 1)
    idx_out = jnp.zeros((tm, LANES), jnp.int32)
    gate_out = jnp.zeros((tm, LANES), f32)
    pos_out = jnp.zeros((tm, LANES), jnp.int32)
    for k in range(TOP_K):
        pos = jnp.sum(jnp.where(sels[k], rank, 0.0), axis=-1, keepdims=True).astype(jnp.int32)
        idx_out = jnp.where(out_lane == k, idxs[k], idx_out)
        gate_out = jnp.where(out_lane == k, exps[k] / denom, gate_out)
        pos_out = jnp.where(out_lane == k, pos, pos_out)
    idx_ref[...] = idx_out
    gate_ref[...] = gate_out
    pos_ref[...] = pos_out


def _mix_router(oa, ob, x, wo, layer, ga, gb, lg, lb, wr, br, alpha):
    T, D = x.shape
    W = oa.shape[1]
    tm = min(ROUTER_ROWS, T)
    rows = lambda n: pl.BlockSpec((tm, n), lambda i: (i, 0))
    full = lambda a: pl.BlockSpec(a.shape, lambda i: (0,) * a.ndim)
    return pl.pallas_call(
        functools.partial(_mix_router_kernel, alpha),
        name="mix_router",
        grid=(T // tm,),
        in_specs=[rows(W), rows(W), rows(D), pl.BlockSpec((None, D, D), lambda i: (layer, 0, 0)),
                  full(ga), full(gb), full(lg), full(lb), full(wr), full(br)],
        out_specs=[rows(D), rows(LANES), rows(LANES), rows(LANES),
                   pl.BlockSpec((1, N_EXPERTS), lambda i: (0, 0))],
        out_shape=[jax.ShapeDtypeStruct((T, D), f32),
                   jax.ShapeDtypeStruct((T, LANES), jnp.int32),
                   jax.ShapeDtypeStruct((T, LANES), f32),
                   jax.ShapeDtypeStruct((T, LANES), jnp.int32),
                   jax.ShapeDtypeStruct((1, N_EXPERTS), jnp.int32)],
        scratch_shapes=[pltpu.VMEM((1, N_EXPERTS), f32),
                        pltpu.VMEM(wr.shape, bf16), pltpu.VMEM(wr.shape, bf16)],
        compiler_params=_params(1),
    )(oa, ob, x, wo, ga, gb, lg, lb, wr, br)


def _dispatch_kernel(dest_ref, pad_start_ref, pad_len_ref, n_used_ref, x_ref, buf_ref,
                     zeros_ref, sem, zero_sem):
    tm = x_ref.shape[0]
    i = pl.program_id(0)
    base = i * (tm * TOP_K)
    n_blocks = buf_ref.shape[0] // EXPERT_ROWS

    @pl.when(i == 0)
    def _():
        zeros_ref[...] = jnp.zeros_like(zeros_ref)

        def pad_row(e, j):
            return pltpu.make_async_copy(
                zeros_ref.at[pl.ds(0, 1), :], buf_ref.at[pl.ds(pad_start_ref[e] + j, 1), :], zero_sem)

        def unused_block(b):
            rows = pl.ds(pl.multiple_of(b * EXPERT_ROWS, EXPERT_ROWS), EXPERT_ROWS)
            return pltpu.make_async_copy(zeros_ref, buf_ref.at[rows, :], zero_sem)

        def for_each(action):
            def per_expert(e, carry):
                def per_row(j, c):
                    action(pad_row(e, j))
                    return c
                return lax.fori_loop(0, pad_len_ref[e], per_row, carry)
            lax.fori_loop(0, N_EXPERTS, per_expert, 0)

            def per_block(b, carry):
                action(unused_block(b))
                return carry
            lax.fori_loop(n_used_ref[0], n_blocks, per_block, 0)

        for_each(lambda copy: copy.start())
        for_each(lambda copy: copy.wait())

    def start(t, carry):
        for k in range(TOP_K):
            pltpu.make_async_copy(
                x_ref.at[pl.ds(t, 1), :],
                buf_ref.at[pl.ds(dest_ref[base + t * TOP_K + k], 1), :], sem).start()
        return carry

    lax.fori_loop(0, tm, start, 0, unroll=4)
    all_rows = buf_ref.at[pl.ds(0, tm * TOP_K), :]
    pltpu.make_async_copy(all_rows, all_rows, sem).wait()


def _dispatch(x1, dest_flat, pad_start, pad_len, n_used, n_rows):
    T, D = x1.shape
    tm = min(DISPATCH_ROWS, T)
    return pl.pallas_call(
        _dispatch_kernel,
        name="moe_dispatch",
        grid_spec=pltpu.PrefetchScalarGridSpec(
            num_scalar_prefetch=4,
            grid=(T // tm,),
            in_specs=[pl.BlockSpec((tm, D), lambda i, *_: (i, 0))],
            out_specs=pl.BlockSpec(memory_space=pl.ANY),
            scratch_shapes=[pltpu.VMEM((EXPERT_ROWS, D), f32),
                            pltpu.SemaphoreType.DMA(()), pltpu.SemaphoreType.DMA(())]),
        out_shape=jax.ShapeDtypeStruct((n_rows, D), f32),
        compiler_params=_params(1),
    )(dest_flat, pad_start, pad_len, n_used, x1)


def _expert_matmul_kernel(activation, blk_expert_ref, n_used_ref, x_ref, w_ref, b_ref, o_ref, wq_ref):
    i = pl.program_id(0)
    used = i < n_used_ref[0]
    first_of_expert = jnp.logical_or(
        i == 0, blk_expert_ref[i] != blk_expert_ref[jnp.maximum(i - 1, 0)])

    @pl.when(jnp.logical_and(used, first_of_expert))
    def _():
        chunk = math.gcd(WEIGHT_CAST_ROWS, w_ref.shape[0])

        def cast_rows(c, carry):
            rows = pl.ds(pl.multiple_of(c * chunk, chunk), chunk)
            wq_ref[rows, :] = w_ref[rows, :].astype(bf16)
            return carry
        lax.fori_loop(0, w_ref.shape[0] // chunk, cast_rows, 0)

    @pl.when(used)
    def _():
        h = _dot(x_ref[...].astype(bf16), wq_ref[...]) + b_ref[...]
        o_ref[...] = activation(h).astype(o_ref.dtype)

    @pl.when(jnp.logical_not(used))
    def _():
        o_ref[...] = jnp.zeros_like(o_ref)


def _clamped_swiglu(h):
    d_ff = h.shape[1] // 2
    gate = jnp.minimum(h[:, :d_ff], SWIGLU_LIMIT)
    up = jnp.clip(h[:, d_ff:], -SWIGLU_LIMIT, SWIGLU_LIMIT)
    return (up + 1.0) * (gate * jax.nn.sigmoid(gate * SWIGLU_ALPHA))


def _expert_matmul(x_buf, blk_expert, n_used, w, b, layer, activation, out_dtype, name):
    R, K = x_buf.shape
    L, E, _, N = w.shape
    n_out = jax.eval_shape(activation, jax.ShapeDtypeStruct((EXPERT_ROWS, N), f32)).shape[1]
    tm = EXPERT_ROWS
    used = lambda i, be, nu: jnp.minimum(i, nu[0] - 1)
    expert = lambda i, be, nu: (layer, be[i], 0, 0)
    return pl.pallas_call(
        functools.partial(_expert_matmul_kernel, activation),
        name=name,
        grid_spec=pltpu.PrefetchScalarGridSpec(
            num_scalar_prefetch=2,
            grid=(R // tm,),
            in_specs=[pl.BlockSpec((tm, K), lambda i, be, nu: (used(i, be, nu), 0)),
                      pl.BlockSpec((None, None, K, N), expert),
                      pl.BlockSpec((None, None, 1, N), expert)],
            out_specs=pl.BlockSpec((tm, n_out), lambda i, be, nu: (i, 0)),
            scratch_shapes=[pltpu.VMEM((K, N), bf16)]),
        out_shape=jax.ShapeDtypeStruct((R, n_out), out_dtype),
        compiler_params=_params(1),
    )(blk_expert, n_used, x_buf, w, b.reshape(L, E, 1, N))


def _expert_ffn(x_buf, blk_expert, n_used, wgu, bgu, wd, bd, layer):
    act = _expert_matmul(x_buf, blk_expert, n_used, wgu, bgu, layer, _clamped_swiglu, bf16,
                         "expert_gate_up")
    return _expert_matmul(act, blk_expert, n_used, wd, bd, layer, lambda h: h, f32, "expert_down")


def _combine_kernel(alpha, dest_ref, y_hbm, x_ref, gate_ref, lg_ref, lb_ref, o_ref, rows_buf, sems):
    tm = x_ref.shape[0]
    i = pl.program_id(0)
    n = pl.num_programs(0)

    def start_tile(tile, slot):
        def body(t, carry):
            for k in range(TOP_K):
                pltpu.make_async_copy(
                    y_hbm.at[pl.ds(dest_ref[(tile * tm + t) * TOP_K + k], 1), :],
                    rows_buf.at[slot, k, pl.ds(t, 1), :], sems.at[slot]).start()
            return carry
        lax.fori_loop(0, tm, body, 0, unroll=4)

    slot = i % 2

    @pl.when(i == 0)
    def _():
        start_tile(0, 0)

    @pl.when(i + 1 < n)
    def _():
        start_tile(i + 1, 1 - slot)

    for k in range(TOP_K):
        pltpu.make_async_copy(y_hbm.at[pl.ds(0, tm), :], rows_buf.at[slot, k], sems.at[slot]).wait()

    y = jnp.zeros(x_ref.shape, f32)
    for k in range(TOP_K):
        y = y + gate_ref[:, k:k + 1] * rows_buf[slot, k]
    o_ref[...] = _layer_norm(alpha * x_ref[...] + y, lg_ref[...], lb_ref[...])


def _combine(y_buf, dest_flat, x1, gates, lg, lb, alpha):
    T, D = x1.shape
    tm = min(COMBINE_ROWS, T)
    return pl.pallas_call(
        functools.partial(_combine_kernel, alpha),
        name="moe_combine",
        grid_spec=pltpu.PrefetchScalarGridSpec(
            num_scalar_prefetch=1,
            grid=(T // tm,),
            in_specs=[pl.BlockSpec(memory_space=pl.ANY),
                      pl.BlockSpec((tm, D), lambda i, d: (i, 0)),
                      pl.BlockSpec((tm, LANES), lambda i, d: (i, 0)),
                      pl.BlockSpec((1, D), lambda i, d: (0, 0)),
                      pl.BlockSpec((1, D), lambda i, d: (0, 0))],
            out_specs=pl.BlockSpec((tm, D), lambda i, d: (i, 0)),
            scratch_shapes=[pltpu.VMEM((2, TOP_K, tm, D), f32),
                            pltpu.SemaphoreType.DMA((2,))]),
        out_shape=jax.ShapeDtypeStruct((T, D), f32),
        compiler_params=_params(1),
    )(dest_flat, y_buf, x1, gates, lg, lb)


def _moe(x1, idx, gates, pos, counts, wgu, bgu, wd, bd, layer, lg, lb, alpha):
    T, D = x1.shape
    n_assign = T * TOP_K
    counts = counts.reshape(N_EXPERTS)
    padded = (counts + EXPERT_ROWS - 1) // EXPERT_ROWS * EXPERT_ROWS
    group_end = jnp.cumsum(padded)
    group_start = group_end - padded
    experts = jnp.arange(N_EXPERTS, dtype=jnp.int32)
    start_of = jnp.sum(jnp.where(idx[:, :TOP_K, None] == experts, group_start, 0), axis=-1)
    dest = (start_of + pos[:, :TOP_K]).reshape(n_assign).astype(jnp.int32)
    n_blocks = -(-n_assign // EXPERT_ROWS) + N_EXPERTS
    block_row = jnp.arange(n_blocks, dtype=jnp.int32)[:, None] * EXPERT_ROWS
    blk_expert = jnp.minimum(jnp.sum(group_end[None, :] <= block_row, axis=-1),
                             N_EXPERTS - 1).astype(jnp.int32)
    n_used = (group_end[-1:] // EXPERT_ROWS).astype(jnp.int32)
    x_buf = _dispatch(x1, dest, (group_start + counts).astype(jnp.int32),
                      (padded - counts).astype(jnp.int32), n_used, n_blocks * EXPERT_ROWS)
    y_buf = _expert_ffn(x_buf, blk_expert, n_used, wgu, bgu, wd, bd, layer)
    return _combine(y_buf, dest, x1, gates, lg, lb, alpha)


def kernel(x, w_in, g_mix_a, g_mix_b, w_out, ln1_g, ln1_b, w_router, b_router,
           w_gate_up, b_gate_up, w_down, b_down, ln2_g, ln2_b):
    B, S, D = x.shape
    depth = w_in.shape[0]
    wa = g_mix_a.shape[1]
    wb = g_mix_b.shape[1]
    heads_a = wa // HEAD_DIM
    heads_b = wb // HEAD_DIM
    alpha = (2.0 * depth) ** 0.25
    slopes = jnp.asarray([2.0 ** (-8.0 * (h + 1) / heads_a) for h in range(heads_a)], f32)
    row = lambda a: a.reshape(1, -1)

    assert wa == wb, "the q/k/v projection is split into two equal column halves"
    w_in_b = w_in.astype(bf16)
    w_out_b = w_out.astype(bf16)

    xt = x.reshape(B * S, D)
    for l in range(depth):
        ha = _proj(xt, w_in_b, l, 0, f32).reshape(B, S, 3 * wa)
        hb = _proj(xt, w_in_b, l, 1, bf16).reshape(B, S, 3 * wb)
        oa = _dilated_attention(ha, slopes, heads_a).reshape(B * S, wa)
        ob = _stick_attention(hb, heads_b).reshape(B * S, wb)
        x1, idx, gates, pos, counts = _mix_router(
            oa, ob, xt, w_out_b, l, row(g_mix_a[l]), row(g_mix_b[l]),
            row(ln1_g[l]), row(ln1_b[l]), w_router[l], row(b_router[l]), alpha)
        xt = _moe(x1, idx, gates, pos, counts, w_gate_up, b_gate_up, w_down, b_down, l,
                  row(ln2_g[l]), row(ln2_b[l]), alpha)
    return xt.reshape(B, S, D)
```
